```python
import jax, jax.numpy as jnp
from jax import lax
import numpy as np

D_MODEL = 1024
BATCH = 8
SEQ = 2048
DEPTH = 2
DEC_BATCH = 16
DEC_SEQ = 16
PAST_LEN = 4096

CHUNK = 64
N_MIXERS = 2
N_HEADS = 8
HEAD_K = 128
HEAD_V = D_MODEL // N_HEADS
D_KEY = N_HEADS * HEAD_K
HGRN_BLOCK = 16
CONV_W = 31
D_FF = 4 * D_MODEL
N_A = (DEPTH + 1) // 2
N_B = DEPTH // 2
ALPHA = (2 * DEPTH) ** 0.25
BETA = (8 * DEPTH) ** -0.25
LN_EPS = 1e-5
RMS_EPS = 1e-5

kernel_name = "hgrn2_conformer_conv_streaming_step"


def layer_norm(x, g, b):
    xf = x.astype(jnp.float32)
    mu = jnp.mean(xf, -1, keepdims=True)
    var = jnp.mean(jnp.square(xf - mu), -1, keepdims=True)
    return ((xf - mu) * lax.rsqrt(var + LN_EPS) * g + b).astype(x.dtype)


def swiglu(x, w_gate, w_up, w_down):
    return (jax.nn.silu(x @ w_gate) * (x @ w_up)) @ w_down


def hgrn2_recurrence(q, k, v, logf, s0):
    bsz, t, h, _ = q.shape
    dv = v.shape[-1]
    n = -(-t // HGRN_BLOCK)
    pad = n * HGRN_BLOCK - t
    padw = ((0, 0), (0, pad), (0, 0), (0, 0))
    q, k, v, logf = (jnp.pad(a.astype(jnp.float32), padw).reshape(bsz, n, HGRN_BLOCK, h, a.shape[-1])
                     for a in (q, k, v, logf))
    b = jnp.cumsum(logf, axis=2)
    ref = b[:, :, HGRN_BLOCK // 2:HGRN_BLOCK // 2 + 1]
    q_rel = q * jnp.exp(b - ref)
    k_rel = k * jnp.exp(ref - b)
    scores = jnp.einsum('bnthk,bnshk->bnhts', q_rel, k_rel)
    mask = jnp.tril(jnp.ones((HGRN_BLOCK, HGRN_BLOCK), dtype=bool))
    scores = jnp.where(mask, scores, 0.0)
    o_intra = jnp.einsum('bnhts,bnshv->bnthv', scores, v)
    q_inter = q * jnp.exp(b)
    b_last = b[:, :, -1]
    k_state = k * jnp.exp(b_last[:, :, None] - b)

    def step(s, inp):
        qi, ki, vi, dl = inp
        o = jnp.einsum('bthk,bhkv->bthv', qi, s)
        s = s * jnp.exp(dl)[..., None] + jnp.einsum('bthk,bthv->bhkv', ki, vi)
        return s, o

    xs = (jnp.moveaxis(q_inter, 1, 0), jnp.moveaxis(k_state, 1, 0),
          jnp.moveaxis(v, 1, 0), jnp.moveaxis(b_last, 1, 0))
    s_final, o_inter = lax.scan(step, s0.astype(jnp.float32), xs)
    o = o_intra + jnp.moveaxis(o_inter, 0, 1)
    o = o.reshape(bsz, n * HGRN_BLOCK, h, dv)[:, :t]
    return o, s_final


def hgrn2_mixer(x, s0, lb, w_in, norm_g, w_out):
    bsz, t, _ = x.shape
    q, f, v, g = jnp.split(x @ w_in, [D_KEY, 2 * D_KEY, 2 * D_KEY + D_MODEL], axis=-1)
    q = jax.nn.silu(q.astype(jnp.float32)) * HEAD_K ** -0.5
    forget = lb + (1.0 - lb) * jax.nn.sigmoid(f.astype(jnp.float32))
    o, s = hgrn2_recurrence(q.reshape(bsz, t, N_HEADS, HEAD_K),
                            (1.0 - forget).reshape(bsz, t, N_HEADS, HEAD_K),
                            v.reshape(bsz, t, N_HEADS, HEAD_V),
                            jnp.log(forget).reshape(bsz, t, N_HEADS, HEAD_K), s0)
    o = o * lax.rsqrt(jnp.mean(jnp.square(o), -1, keepdims=True) + RMS_EPS) * norm_g
    o = o.reshape(bsz, t, D_MODEL) * jax.nn.silu(g.astype(jnp.float32))
    return o.astype(x.dtype) @ w_out, s


def conv_mixer(x, cache, w_pw1, b_pw1, w_dw, b_dw, ln_g, ln_b, w_pw2, b_pw2):
    a, gate = jnp.split(x @ w_pw1 + b_pw1, 2, axis=-1)
    u = a * jax.nn.sigmoid(gate)
    u_ext = jnp.concatenate([cache.astype(u.dtype), u], axis=1)
    y = lax.conv_general_dilated(u_ext, w_dw[:, None, :].astype(u.dtype), window_strides=(1,),
                                 padding='VALID', dimension_numbers=('NWC', 'WIO', 'NWC'),
                                 feature_group_count=D_MODEL) + b_dw
    new_cache = u_ext[:, -(CONV_W - 1):]
    y = jax.nn.silu(layer_norm(y, ln_g, ln_b))
    return y @ w_pw2 + b_pw2, new_cache


def trunk(x, hgrn_state, conv_cache, w):
    new_h, new_c = [], []
    lb_all = jnp.cumsum(jax.nn.softmax(w['hgrn_lb'].astype(jnp.float32), axis=0), axis=0)
    for i in range(DEPTH):
        j = i // N_MIXERS
        x = layer_norm(ALPHA * x + 0.5 * swiglu(x, w['ffn_w_gate'][i, 0], w['ffn_w_up'][i, 0], w['ffn_w_down'][i, 0]),
                       w['ln_g'][i, 0], w['ln_b'][i, 0])
        if i % N_MIXERS == 0:
            m, s = hgrn2_mixer(x, hgrn_state[j], lb_all[i], w['hgrn_w_in'][j], w['hgrn_norm_g'][j], w['hgrn_w_out'][j])
            new_h.append(s.astype(x.dtype))
        else:
            m, c = conv_mixer(x, conv_cache[j], w['conv_w_pw1'][j], w['conv_b_pw1'][j], w['conv_w_dw'][j],
                              w['conv_b_dw'][j], w['conv_ln_g'][j], w['conv_ln_b'][j],
                              w['conv_w_pw2'][j], w['conv_b_pw2'][j])
            new_c.append(c)
        x = layer_norm(ALPHA * x + m, w['ln_g'][i, 1], w['ln_b'][i, 1])
        x = layer_norm(ALPHA * x + 0.5 * swiglu(x, w['ffn_w_gate'][i, 1], w['ffn_w_up'][i, 1], w['ffn_w_down'][i, 1]),
                       w['ln_g'][i, 2], w['ln_b'][i, 2])
    return x, jnp.stack(new_h), jnp.stack(new_c)


def setup_inputs(seed: int = 0) -> dict:
    key = jax.random.key(seed)
    ks = jax.random.split(key, 24)

    def nrm(k, shape, scale):
        return jax.random.normal(k, shape, jnp.float32) * scale

    return {
        'x_prompt': nrm(ks[0], (BATCH, SEQ, D_MODEL), 1.0),
        'x_sample': nrm(ks[1], (DEC_BATCH, DEC_SEQ, D_MODEL), 1.0),
        'state_hgrn': nrm(ks[2], (N_A, DEC_BATCH, N_HEADS, HEAD_K, HEAD_V), 0.5),
        'cache_conv': nrm(ks[3], (N_B, DEC_BATCH, CONV_W - 1, D_MODEL), 0.5),
        'ffn_w_gate': nrm(ks[4], (DEPTH, 2, D_MODEL, D_FF), D_MODEL ** -0.5),
        'ffn_w_up': nrm(ks[5], (DEPTH, 2, D_MODEL, D_FF), D_MODEL ** -0.5),
        'ffn_w_down': nrm(ks[6], (DEPTH, 2, D_FF, D_MODEL), BETA * D_FF ** -0.5),
        'ln_g': 1.0 + nrm(ks[7], (DEPTH, 3, D_MODEL), 0.02),
        'ln_b': nrm(ks[8], (DEPTH, 3, D_MODEL), 0.02),
        'hgrn_w_in': nrm(ks[9], (N_A, D_MODEL, 2 * D_KEY + 2 * D_MODEL), D_MODEL ** -0.5),
        'hgrn_lb': nrm(ks[10], (DEPTH + 1, D_KEY), 1.0),
        'hgrn_norm_g': 1.0 + nrm(ks[11], (N_A, HEAD_V), 0.02),
        'hgrn_w_out': nrm(ks[12], (N_A, D_MODEL, D_MODEL), BETA * D_MODEL ** -0.5),
        'conv_w_pw1': nrm(ks[13], (N_B, D_MODEL, 2 * D_MODEL), D_MODEL ** -0.5),
        'conv_b_pw1': nrm(ks[14], (N_B, 2 * D_MODEL), 0.02),
        'conv_w_dw': nrm(ks[15], (N_B, CONV_W, D_MODEL), CONV_W ** -0.5),
        'conv_b_dw': nrm(ks[16], (N_B, D_MODEL), 0.02),
        'conv_ln_g': 1.0 + nrm(ks[17], (N_B, D_MODEL), 0.02),
        'conv_ln_b': nrm(ks[18], (N_B, D_MODEL), 0.02),
        'conv_w_pw2': nrm(ks[19], (N_B, D_MODEL, D_MODEL), BETA * D_MODEL ** -0.5),
        'conv_b_pw2': nrm(ks[20], (N_B, D_MODEL), 0.02),
    }


def reference(x_prompt, x_sample, state_hgrn, cache_conv, ffn_w_gate, ffn_w_up, ffn_w_down, ln_g, ln_b,
              hgrn_w_in, hgrn_lb, hgrn_norm_g, hgrn_w_out, conv_w_pw1, conv_b_pw1, conv_w_dw, conv_b_dw,
              conv_ln_g, conv_ln_b, conv_w_pw2, conv_b_pw2):
    w = {'ffn_w_gate': ffn_w_gate, 'ffn_w_up': ffn_w_up, 'ffn_w_down': ffn_w_down, 'ln_g': ln_g, 'ln_b': ln_b,
         'hgrn_w_in': hgrn_w_in, 'hgrn_lb': hgrn_lb, 'hgrn_norm_g': hgrn_norm_g, 'hgrn_w_out': hgrn_w_out,
         'conv_w_pw1': conv_w_pw1, 'conv_b_pw1': conv_b_pw1, 'conv_w_dw': conv_w_dw, 'conv_b_dw': conv_b_dw,
         'conv_ln_g': conv_ln_g, 'conv_ln_b': conv_ln_b, 'conv_w_pw2': conv_w_pw2, 'conv_b_pw2': conv_b_pw2}
    b_p = x_prompt.shape[0]
    h0 = jnp.zeros((N_A, b_p, N_HEADS, HEAD_K, HEAD_V), x_prompt.dtype)
    c0 = jnp.zeros((N_B, b_p, CONV_W - 1, D_MODEL), x_prompt.dtype)
    y_prompt, hgrn_state_prompt, conv_cache_prompt = trunk(x_prompt, h0, c0, w)
    y_sample, hgrn_state_sample, conv_cache_sample = trunk(x_sample, state_hgrn, cache_conv, w)
    return (y_prompt, y_sample, hgrn_state_prompt, hgrn_state_sample, conv_cache_prompt, conv_cache_sample)
```

```python
import functools

import jax
import jax.numpy as jnp
from jax import lax
from jax.experimental import pallas as pl
from jax.experimental.pallas import tpu as pltpu

F32 = jnp.float32
BF16 = jnp.bfloat16

DEPTH = 2
N_HEADS = 8
HEAD_K = 128
HEAD_V = 128
CONV_W = 31
HGRN_BLOCK = 16
ALPHA = (2 * DEPTH) ** 0.25
LN_EPS = 1e-5
RMS_EPS = 1e-5

V7X_VMEM_LIMIT_BYTES = 56 * 1024 * 1024
CONV_HIST = 32


def _row_tile(m):
    return 512 if m % 512 == 0 else m


def _resident(shape):
    return pl.BlockSpec(shape, lambda *_: (0,) * len(shape), pipeline_mode=pl.Buffered(1))


def _params(n_axes, semantics=None):
    return pltpu.CompilerParams(
        dimension_semantics=semantics or ("arbitrary",) * n_axes,
        vmem_limit_bytes=V7X_VMEM_LIMIT_BYTES)


def _ln(y, g, b):
    mu = jnp.mean(y, -1, keepdims=True)
    d = y - mu
    var = jnp.mean(d * d, -1, keepdims=True)
    return d * lax.rsqrt(var + LN_EPS) * g + b


def _dot(a, b):
    return jnp.dot(a, b, preferred_element_type=F32)


def _dot_nt(a, b):
    return lax.dot_general(a, b, (((1,), (1,)), ((), ())), preferred_element_type=F32)


def _dot_tn(a, b):
    return lax.dot_general(a, b, (((0,), (0,)), ((), ())), preferred_element_type=F32)


def _ffn_ln_kernel(x_ref, wg_ref, wu_ref, wd_ref, g_ref, b_ref, o_ref, h_ref, *, fc):
    x = x_ref[...]
    xb = x.astype(BF16)
    d_ff = wg_ref.shape[1]
    for c in range(d_ff // fc):
        sl = slice(c * fc, (c + 1) * fc)
        gate = _dot(xb, wg_ref[:, sl])
        up = _dot(xb, wu_ref[:, sl])
        h_ref[:, sl] = (gate * jax.nn.sigmoid(gate) * up).astype(BF16)
    y = _dot(h_ref[...], wd_ref[...])
    o_ref[...] = _ln(ALPHA * x + 0.5 * y, g_ref[...], b_ref[...])


def _ffn_ln(x, wg, wu, wd, g, b):
    m, d = x.shape
    d_ff = wg.shape[1]
    tm = _row_tile(m)
    return pl.pallas_call(
        functools.partial(_ffn_ln_kernel, fc=512),
        grid=(m // tm,),
        in_specs=[pl.BlockSpec((tm, d), lambda i: (i, 0)),
                  _resident((d, d_ff)), _resident((d, d_ff)), _resident((d_ff, d)),
                  _resident((1, d)), _resident((1, d))],
        out_specs=pl.BlockSpec((tm, d), lambda i: (i, 0)),
        out_shape=jax.ShapeDtypeStruct((m, d), F32),
        scratch_shapes=[pltpu.VMEM((tm, d_ff), BF16)],
        compiler_params=_params(1),
        name="ffn_ln",
    )(x, wg, wu, wd, g.reshape(1, d), b.reshape(1, d))


def _hgrn_in_kernel(x_ref, w_ref, lb_ref, qs_ref, lf_ref, k_ref, v_ref, sg_ref, *, layer):
    dk = qs_ref.shape[1]
    dv = v_ref.shape[1]
    xb = x_ref[...].astype(BF16)
    lbp = lb_ref[...]
    e = jnp.exp(lbp - jnp.max(lbp, axis=0, keepdims=True))
    lb = jnp.sum(e[:layer + 1], axis=0, keepdims=True) / jnp.sum(e, axis=0, keepdims=True)

    q = _dot(xb, w_ref[:, 0:dk])
    qs_ref[...] = q * jax.nn.sigmoid(q) * (HEAD_K ** -0.5)
    f = _dot(xb, w_ref[:, dk:2 * dk])
    forget = lb + (1.0 - lb) * jax.nn.sigmoid(f)
    k_ref[...] = 1.0 - forget
    lf_ref[...] = jnp.log(forget)
    v_ref[...] = _dot(xb, w_ref[:, 2 * dk:2 * dk + dv]).astype(BF16)
    g = _dot(xb, w_ref[:, 2 * dk + dv:])
    sg_ref[...] = g * jax.nn.sigmoid(g)


def _hgrn_in(x, w_in, lb_param, layer):
    m, d = x.shape
    dk = N_HEADS * HEAD_K
    tm = _row_tile(m)
    row = lambda i: (i, 0)
    return pl.pallas_call(
        functools.partial(_hgrn_in_kernel, layer=layer),
        grid=(m // tm,),
        in_specs=[pl.BlockSpec((tm, d), row), _resident(w_in.shape), _resident(lb_param.shape)],
        out_specs=[pl.BlockSpec((tm, dk), row), pl.BlockSpec((tm, dk), row), pl.BlockSpec((tm, dk), row),
                   pl.BlockSpec((tm, d), row), pl.BlockSpec((tm, d), row)],
        out_shape=[jax.ShapeDtypeStruct((m, dk), F32), jax.ShapeDtypeStruct((m, dk), F32),
                   jax.ShapeDtypeStruct((m, dk), F32), jax.ShapeDtypeStruct((m, d), BF16),
                   jax.ShapeDtypeStruct((m, d), F32)],
        compiler_params=_params(1),
        name="hgrn_in",
    )(x, w_in, lb_param)


def _bcast_rows(b, first, period, n):
    parts = [jnp.broadcast_to(b[g * period + first:g * period + first + 1, :], (period, b.shape[1]))
             for g in range(n // period)]
    return parts[0] if len(parts) == 1 else jnp.concatenate(parts, axis=0)


def _hgrn_rec_kernel(*refs, chunk, has_init):
    if has_init:
        qs_ref, lf_ref, k_ref, v_ref, sg_ref, ng_ref, s0_ref, o_ref, s_out_ref, p_ref = refs
    else:
        qs_ref, lf_ref, k_ref, v_ref, sg_ref, ng_ref, o_ref, s_out_ref, p_ref = refs
        s0_ref = None
    c = pl.program_id(1)
    n = chunk

    @pl.when(c == 0)
    def _():
        for h in range(N_HEADS):
            p_ref[h] = s0_ref[0, h].T if has_init else jnp.zeros((HEAD_V, HEAD_K), F32)

    qs = qs_ref[...]
    lf = lf_ref[...]
    k = k_ref[...]

    t_idx = lax.broadcasted_iota(jnp.int32, (n, n), 0)
    s_idx = lax.broadcasted_iota(jnp.int32, (n, n), 1)
    tri = (s_idx <= t_idx).astype(BF16)
    lf_hi = lf.astype(BF16)
    r1 = lf - lf_hi.astype(F32)
    lf_mid = r1.astype(BF16)
    lf_lo = (r1 - lf_mid.astype(F32)).astype(BF16)
    b = _dot(tri, lf_hi) + _dot(tri, lf_mid) + _dot(tri, lf_lo)

    levels = []
    l = n // 2
    while l >= HGRN_BLOCK:
        levels.append(l)
        l //= 2
    row = lax.broadcasted_iota(jnp.int32, (n, 1), 0)
    x_lv, m_lv = [], []
    for l in levels:
        anchor = _bcast_rows(b, l - 1, 2 * l, n)
        is_right = ((row // l) & 1) == 1
        x_lv.append((jnp.where(is_right, qs, k) * jnp.exp(-jnp.abs(b - anchor))).astype(BF16))
        m_lv.append((((t_idx // l) & 1) == 1) & ((s_idx // l) == (t_idx // l) - 1))
    mid = _bcast_rows(b, HGRN_BLOCK // 2, HGRN_BLOCK, n)
    q_d = (qs * jnp.exp(b - mid)).astype(BF16)
    k_d = (k * jnp.exp(mid - b)).astype(BF16)
    m_d = ((t_idx // HGRN_BLOCK) == (s_idx // HGRN_BLOCK)) & (s_idx <= t_idx)

    b_last = b[n - 1:n, :]
    q_in = (qs * jnp.exp(b)).astype(BF16)
    k_st = (k * jnp.exp(b_last - b)).astype(BF16)
    decay = jnp.exp(b_last)
    v = v_ref[...]
    sg = sg_ref[...]
    ng = ng_ref[...]

    for h in range(N_HEADS):
        sl = slice(h * HEAD_K, (h + 1) * HEAD_K)
        sv = slice(h * HEAD_V, (h + 1) * HEAD_V)
        scores = jnp.where(m_d, _dot_nt(q_d[:, sl], k_d[:, sl]), 0.0)
        for x_l, m_l in zip(x_lv, m_lv):
            scores = jnp.where(m_l, _dot_nt(x_l[:, sl], x_l[:, sl]), scores)
        p = p_ref[h]
        o = _dot(scores.astype(BF16), v[:, sv]) + _dot_nt(q_in[:, sl], p.astype(BF16))
        p_ref[h] = p * decay[:, sl] + _dot_tn(v[:, sv], k_st[:, sl])
        o = o * lax.rsqrt(jnp.mean(o * o, -1, keepdims=True) + RMS_EPS) * ng
        o_ref[:, sv] = (o * sg[:, sv]).astype(BF16)

    @pl.when(c == pl.num_programs(1) - 1)
    def _():
        for h in range(N_HEADS):
            s_out_ref[0, h] = p_ref[h].T


def _hgrn_rec(qs, lf, k, v, sg, norm_g, s0, bsz, t):
    m, d = v.shape
    dk = qs.shape[1]
    chunk = min(t, 128)
    nc = t // chunk
    row = lambda bi, ci: (bi * nc + ci, 0)
    in_specs = [pl.BlockSpec((chunk, dk), row), pl.BlockSpec((chunk, dk), row), pl.BlockSpec((chunk, dk), row),
                pl.BlockSpec((chunk, d), row), pl.BlockSpec((chunk, d), row), _resident((1, HEAD_V))]
    args = [qs, lf, k, v, sg, norm_g.reshape(1, HEAD_V)]
    if s0 is not None:
        in_specs.append(pl.BlockSpec((1, N_HEADS, HEAD_K, HEAD_V), lambda bi, ci: (bi, 0, 0, 0)))
        args.append(s0)
    return pl.pallas_call(
        functools.partial(_hgrn_rec_kernel, chunk=chunk, has_init=s0 is not None),
        grid=(bsz, nc),
        in_specs=in_specs,
        out_specs=[pl.BlockSpec((chunk, d), row),
                   pl.BlockSpec((1, N_HEADS, HEAD_K, HEAD_V), lambda bi, ci: (bi, 0, 0, 0))],
        out_shape=[jax.ShapeDtypeStruct((m, d), BF16),
                   jax.ShapeDtypeStruct((bsz, N_HEADS, HEAD_K, HEAD_V), F32)],
        scratch_shapes=[pltpu.VMEM((N_HEADS, HEAD_V, HEAD_K), F32)],
        compiler_params=_params(2),
        name="hgrn_rec",
    )(*args)


def _proj_res_ln_kernel(a_ref, res_ref, w_ref, g_ref, b_ref, o_ref):
    y = _dot(a_ref[...], w_ref[...])
    o_ref[...] = _ln(ALPHA * res_ref[...] + y, g_ref[...], b_ref[...])


def _proj_res_ln(a, res, w, g, b):
    m, d = res.shape
    tm = _row_tile(m)
    row = lambda i: (i, 0)
    return pl.pallas_call(
        _proj_res_ln_kernel,
        grid=(m // tm,),
        in_specs=[pl.BlockSpec((tm, a.shape[1]), row), pl.BlockSpec((tm, d), row),
                  _resident(w.shape), _resident((1, d)), _resident((1, d))],
        out_specs=pl.BlockSpec((tm, d), row),
        out_shape=jax.ShapeDtypeStruct((m, d), F32),
        compiler_params=_params(1),
        name="proj_res_ln",
    )(a, res, w, g.reshape(1, d), b.reshape(1, d))


def _glu_kernel(x_ref, w_ref, b_ref, u_ref):
    d = u_ref.shape[1]
    xb = x_ref[...].astype(BF16)
    a = _dot(xb, w_ref[:, 0:d]) + b_ref[:, 0:d]
    gate = _dot(xb, w_ref[:, d:]) + b_ref[:, d:]
    u_ref[...] = a * jax.nn.sigmoid(gate)


def _glu(x, w, b):
    m, d = x.shape
    tm = _row_tile(m)
    row = lambda i: (i, 0)
    return pl.pallas_call(
        _glu_kernel,
        grid=(m // tm,),
        in_specs=[pl.BlockSpec((tm, d), row), _resident(w.shape), _resident((1, 2 * d))],
        out_specs=pl.BlockSpec((tm, d), row),
        out_shape=jax.ShapeDtypeStruct((m, d), F32),
        compiler_params=_params(1),
        name="conv_glu",
    )(x, w, b.reshape(1, 2 * d))


def _conv_kernel(*refs, tt, has_cache):
    if has_cache:
        (u_ref, cache_ref, x_ref, wdw_ref, bdw_ref, lng_ref, lnb_ref, w2_ref, b2_ref, g_ref, b_ref,
         o_ref, nc_ref, ext_ref, act_ref) = refs
    else:
        (u_ref, x_ref, wdw_ref, bdw_ref, lng_ref, lnb_ref, w2_ref, b2_ref, g_ref, b_ref,
         o_ref, nc_ref, ext_ref, act_ref) = refs
    ti = pl.program_id(1)
    d = u_ref.shape[1]
    pad = CONV_HIST - (CONV_W - 1)

    @pl.when(ti == 0)
    def _():
        ext_ref[0:CONV_HIST, :] = jnp.zeros((CONV_HIST, d), F32)
        if has_cache:
            ext_ref[pad:CONV_HIST, :] = cache_ref[0]

    @pl.when(ti > 0)
    def _():
        ext_ref[0:CONV_HIST, :] = ext_ref[tt:tt + CONV_HIST, :]

    ext_ref[CONV_HIST:CONV_HIST + tt, :] = u_ref[...]

    rb = min(tt, 32)
    for r0 in range(0, tt, rb):
        acc = jnp.zeros((rb, d), F32)
        for j in range(CONV_W):
            acc = acc + wdw_ref[j:j + 1, :] * ext_ref[r0 + pad + j:r0 + pad + j + rb, :]
        y = _ln(acc + bdw_ref[...], lng_ref[...], lnb_ref[...])
        act_ref[r0:r0 + rb, :] = (y * jax.nn.sigmoid(y)).astype(BF16)
    mix = _dot(act_ref[...], w2_ref[...]) + b2_ref[...]
    o_ref[...] = _ln(ALPHA * x_ref[...] + mix, g_ref[...], b_ref[...])

    @pl.when(ti == pl.num_programs(1) - 1)
    def _():
        nc_ref[0] = ext_ref[tt + pad:tt + CONV_HIST, :]


def _conv_mix(u, cache, x, wdw, bdw, lng, lnb, w2, b2, g, b, bsz, t):
    m, d = u.shape
    tt = min(t, 512)
    nt = t // tt
    row = lambda bi, ti: (bi * nt + ti, 0)
    vec = lambda a: a.reshape(1, d)
    in_specs = [pl.BlockSpec((tt, d), row)]
    args = [u]
    if cache is not None:
        in_specs.append(pl.BlockSpec((1, CONV_W - 1, d), lambda bi, ti: (bi, 0, 0)))
        args.append(cache)
    in_specs += [pl.BlockSpec((tt, d), row), _resident(wdw.shape), _resident((1, d)), _resident((1, d)),
                 _resident((1, d)), _resident(w2.shape), _resident((1, d)), _resident((1, d)), _resident((1, d))]
    args += [x, wdw, vec(bdw), vec(lng), vec(lnb), w2, vec(b2), vec(g), vec(b)]
    return pl.pallas_call(
        functools.partial(_conv_kernel, tt=tt, has_cache=cache is not None),
        grid=(bsz, nt),
        in_specs=in_specs,
        out_specs=[pl.BlockSpec((tt, d), row),
                   pl.BlockSpec((1, CONV_W - 1, d), lambda bi, ti: (bi, 0, 0))],
        out_shape=[jax.ShapeDtypeStruct((m, d), F32),
                   jax.ShapeDtypeStruct((bsz, CONV_W - 1, d), F32)],
        scratch_shapes=[pltpu.VMEM((CONV_HIST + tt, d), F32), pltpu.VMEM((tt, d), BF16)],
        compiler_params=_params(2),
        name="conv_mix",
    )(*args)


def _trunk(x3, hgrn_state, conv_cache, w):
    bsz, t, d = x3.shape
    x = x3.reshape(bsz * t, d)
    new_h, new_c = [], []
    for i in range(DEPTH):
        j = i // 2
        x = _ffn_ln(x, w['ffn_w_gate'][i, 0], w['ffn_w_up'][i, 0], w['ffn_w_down'][i, 0],
                    w['ln_g'][i, 0], w['ln_b'][i, 0])
        if i % 2 == 0:
            qs, lf, k, v, sg = _hgrn_in(x, w['hgrn_w_in'][j], w['hgrn_lb'], i)
            s0 = None if hgrn_state is None else hgrn_state[j]
            og, s = _hgrn_rec(qs, lf, k, v, sg, w['hgrn_norm_g'][j], s0, bsz, t)
            new_h.append(s)
            x = _proj_res_ln(og, x, w['hgrn_w_out'][j], w['ln_g'][i, 1], w['ln_b'][i, 1])
        else:
            u = _glu(x, w['conv_w_pw1'][j], w['conv_b_pw1'][j])
            cache = None if conv_cache is None else conv_cache[j]
            x, cnew = _conv_mix(u, cache, x, w['conv_w_dw'][j], w['conv_b_dw'][j], w['conv_ln_g'][j],
                                w['conv_ln_b'][j], w['conv_w_pw2'][j], w['conv_b_pw2'][j],
                                w['ln_g'][i, 1], w['ln_b'][i, 1], bsz, t)
            new_c.append(cnew)
        x = _ffn_ln(x, w['ffn_w_gate'][i, 1], w['ffn_w_up'][i, 1], w['ffn_w_down'][i, 1],
                    w['ln_g'][i, 2], w['ln_b'][i, 2])
    return x.reshape(bsz, t, d), jnp.stack(new_h), jnp.stack(new_c)


def kernel(x_prompt, x_sample, state_hgrn, cache_conv, ffn_w_gate, ffn_w_up, ffn_w_down, ln_g, ln_b, hgrn_w_in, hgrn_lb, hgrn_norm_g, hgrn_w_out, conv_w_pw1, conv_b_pw1, conv_w_dw, conv_b_dw, conv_ln_g, conv_ln_b, conv_w_pw2, conv_b_pw2):
    w = {'ffn_w_gate': ffn_w_gate.astype(BF16), 'ffn_w_up': ffn_w_up.astype(BF16),
         'ffn_w_down': ffn_w_down.astype(BF16), 'ln_g': ln_g, 'ln_b': ln_b,
         'hgrn_w_in': hgrn_w_in.astype(BF16), 'hgrn_lb': hgrn_lb, 'hgrn_norm_g': hgrn_norm_g,
         'hgrn_w_out': hgrn_w_out.astype(BF16), 'conv_w_pw1': conv_w_pw1.astype(BF16),
         'conv_b_pw1': conv_b_pw1, 'conv_w_dw': conv_w_dw, 'conv_b_dw': conv_b_dw,
         'conv_ln_g': conv_ln_g, 'conv_ln_b': conv_ln_b, 'conv_w_pw2': conv_w_pw2.astype(BF16),
         'conv_b_pw2': conv_b_pw2}
    y_p, h_p, c_p = _trunk(x_prompt, None, None, w)
    y_s, h_s, c_s = _trunk(x_sample, state_hgrn, cache_conv, w)
    return (y_p, y_s, h_p, h_s, c_p, c_s)
```

```python
import functools

import jax
import jax.numpy as jnp
from jax import lax
from jax.experimental import pallas as pl
from jax.experimental.pallas import tpu as pltpu

F32 = jnp.float32
BF16 = jnp.bfloat16

DEPTH = 2
N_HEADS = 8
HEAD_K = 128
HEAD_V = 128
CONV_W = 31
HGRN_BLOCK = 16
ALPHA = (2 * DEPTH) ** 0.25
LN_EPS = 1e-5
RMS_EPS = 1e-5

V7X_VMEM_LIMIT_BYTES = 56 * 1024 * 1024
LANES = 128
SUBLANES = 8
BF16_ROWS = 16
CONV_HIST = 32
CONV_ROWS = 64
HGRN_CHUNK = 128
HGRN_SEQS = 4


def _row_tile(m):
    return 512 if m % 512 == 0 else m


def _resident(shape):
    return pl.BlockSpec(shape, lambda *_: (0,) * len(shape), pipeline_mode=pl.Buffered(1))


def _params(n_axes):
    return pltpu.CompilerParams(
        dimension_semantics=("arbitrary",) * n_axes,
        vmem_limit_bytes=V7X_VMEM_LIMIT_BYTES)


def _ln(y, g, b):
    mu = jnp.mean(y, -1, keepdims=True)
    d = y - mu
    var = jnp.mean(d * d, -1, keepdims=True)
    return d * lax.rsqrt(var + LN_EPS) * g + b


def _dot(a, b):
    return jnp.dot(a, b, preferred_element_type=F32)


def _dot_nt(a, b):
    return lax.dot_general(a, b, (((1,), (1,)), ((), ())), preferred_element_type=F32)


def _dot_tn(a, b):
    return lax.dot_general(a, b, (((0,), (0,)), ((), ())), preferred_element_type=F32)


def _cast_specs(w, lead, steps):
    rows, cols = w.shape[-2:]
    nblk = min(steps, rows // BF16_ROWS)
    rb = rows // nblk
    assert rb * nblk == rows and rb % BF16_ROWS == 0
    blk = lambda i: jnp.minimum(i, nblk - 1)
    in_spec = pl.BlockSpec((None,) * len(lead) + (rb, cols), lambda i: tuple(lead) + (blk(i), 0))
    out_spec = pl.BlockSpec((rb, cols), lambda i: (blk(i), 0))
    return in_spec, out_spec, jax.ShapeDtypeStruct((rows, cols), BF16)


def _ffn_ln_kernel(*refs, fc, n_cast):
    x_ref, wg_ref, wu_ref, wd_ref, g_ref, b_ref = refs[:6]
    cast_in = refs[6:6 + n_cast]
    o_ref = refs[6 + n_cast]
    cast_out = refs[7 + n_cast:7 + 2 * n_cast]
    h_ref = refs[7 + 2 * n_cast]
    x = x_ref[...]
    xb = x.astype(BF16)
    d_ff = wg_ref.shape[1]
    for c in range(d_ff // fc):
        sl = slice(c * fc, (c + 1) * fc)
        gate = _dot(xb, wg_ref[:, sl])
        up = _dot(xb, wu_ref[:, sl])
        h_ref[:, sl] = (gate * jax.nn.sigmoid(gate) * up).astype(BF16)
    y = _dot(h_ref[...], wd_ref[...])
    o_ref[...] = _ln(ALPHA * x + 0.5 * y, g_ref[...], b_ref[...])
    for src, dst in zip(cast_in, cast_out):
        dst[...] = src[...].astype(BF16)


def _ffn_ln(x, wg, wu, wd, g, b, casts=()):
    m, d = x.shape
    d_ff = wg.shape[1]
    tm = _row_tile(m)
    steps = m // tm
    row = lambda i: (i, 0)
    cast_specs = [_cast_specs(w, lead, steps) for w, lead in casts]
    outs = pl.pallas_call(
        functools.partial(_ffn_ln_kernel, fc=512, n_cast=len(casts)),
        grid=(steps,),
        in_specs=[pl.BlockSpec((tm, d), row),
                  _resident((d, d_ff)), _resident((d, d_ff)), _resident((d_ff, d)),
                  _resident((1, d)), _resident((1, d))] + [s[0] for s in cast_specs],
        out_specs=[pl.BlockSpec((tm, d), row)] + [s[1] for s in cast_specs],
        out_shape=[jax.ShapeDtypeStruct((m, d), F32)] + [s[2] for s in cast_specs],
        scratch_shapes=[pltpu.VMEM((tm, d_ff), BF16)],
        compiler_params=_params(1),
        name="ffn_ln",
    )(x, wg, wu, wd, g.reshape(1, d), b.reshape(1, d), *[w for w, _ in casts])
    return outs[0], outs[1:]


def _hgrn_in_kernel(x_ref, w_ref, lb_ref, qs_ref, lf_ref, k_ref, v_ref, sg_ref, *, layer):
    dk = qs_ref.shape[1]
    dv = v_ref.shape[1]
    xb = x_ref[...].astype(BF16)
    lbp = lb_ref[...]
    e = jnp.exp(lbp - jnp.max(lbp, axis=0, keepdims=True))
    lb = jnp.sum(e[:layer + 1], axis=0, keepdims=True) / jnp.sum(e, axis=0, keepdims=True)

    q = _dot(xb, w_ref[:, 0:dk])
    qs_ref[...] = q * jax.nn.sigmoid(q) * (HEAD_K ** -0.5)
    f = _dot(xb, w_ref[:, dk:2 * dk])
    forget = lb + (1.0 - lb) * jax.nn.sigmoid(f)
    k_ref[...] = 1.0 - forget
    lf_ref[...] = jnp.log(forget)
    v_ref[...] = _dot(xb, w_ref[:, 2 * dk:2 * dk + dv]).astype(BF16)
    g = _dot(xb, w_ref[:, 2 * dk + dv:])
    sg_ref[...] = g * jax.nn.sigmoid(g)


def _hgrn_in(x, w_in, lb_param, layer):
    m, d = x.shape
    dk = N_HEADS * HEAD_K
    tm = _row_tile(m)
    row = lambda i: (i, 0)
    return pl.pallas_call(
        functools.partial(_hgrn_in_kernel, layer=layer),
        grid=(m // tm,),
        in_specs=[pl.BlockSpec((tm, d), row), _resident(w_in.shape), _resident(lb_param.shape)],
        out_specs=[pl.BlockSpec((tm, dk), row), pl.BlockSpec((tm, dk), row), pl.BlockSpec((tm, dk), row),
                   pl.BlockSpec((tm, d), row), pl.BlockSpec((tm, d), row)],
        out_shape=[jax.ShapeDtypeStruct((m, dk), F32), jax.ShapeDtypeStruct((m, dk), F32),
                   jax.ShapeDtypeStruct((m, dk), F32), jax.ShapeDtypeStruct((m, d), BF16),
                   jax.ShapeDtypeStruct((m, d), F32)],
        compiler_params=_params(1),
        name="hgrn_in",
    )(x, w_in, lb_param)


def _bcast_rows(b, first, period, n):
    parts = [jnp.broadcast_to(b[g * period + first:g * period + first + 1, :], (period, b.shape[1]))
             for g in range(n // period)]
    return parts[0] if len(parts) == 1 else jnp.concatenate(parts, axis=0)


def _hgrn_rec_kernel(*refs, chunk, n_seq, has_init):
    if has_init:
        qs_ref, lf_ref, k_ref, v_ref, sg_ref, ng_ref, s0_ref, o_ref, s_out_ref, p_ref = refs
    else:
        qs_ref, lf_ref, k_ref, v_ref, sg_ref, ng_ref, o_ref, s_out_ref, p_ref = refs
        s0_ref = None
    c = pl.program_id(1)
    n = chunk

    @pl.when(c == 0)
    def _():
        for s in range(n_seq):
            for h in range(N_HEADS):
                p_ref[s, h] = s0_ref[s, h].T if has_init else jnp.zeros((HEAD_V, HEAD_K), F32)

    t_idx = lax.broadcasted_iota(jnp.int32, (n, n), 0)
    s_idx = lax.broadcasted_iota(jnp.int32, (n, n), 1)
    row = lax.broadcasted_iota(jnp.int32, (n, 1), 0)
    tri = (s_idx <= t_idx).astype(BF16)
    levels = []
    l = n // 2
    while l >= HGRN_BLOCK:
        levels.append(l)
        l //= 2
    m_lv = [(((t_idx // l) & 1) == 1) & ((s_idx // l) == (t_idx // l) - 1) for l in levels]
    right_lv = [((row // l) & 1) == 1 for l in levels]
    m_d = ((t_idx // HGRN_BLOCK) == (s_idx // HGRN_BLOCK)) & (s_idx <= t_idx)
    ng = ng_ref[...]

    for s in range(n_seq):
        qs = qs_ref[s]
        lf = lf_ref[s]
        k = k_ref[s]
        lf_hi = lf.astype(BF16)
        r1 = lf - lf_hi.astype(F32)
        lf_mid = r1.astype(BF16)
        lf_lo = (r1 - lf_mid.astype(F32)).astype(BF16)
        b = _dot(tri, lf_hi) + _dot(tri, lf_mid) + _dot(tri, lf_lo)

        x_lv = []
        for l, is_right in zip(levels, right_lv):
            anchor = _bcast_rows(b, l - 1, 2 * l, n)
            x_lv.append((jnp.where(is_right, qs, k) * jnp.exp(-jnp.abs(b - anchor))).astype(BF16))
        mid = _bcast_rows(b, HGRN_BLOCK // 2, HGRN_BLOCK, n)
        q_d = (qs * jnp.exp(b - mid)).astype(BF16)
        k_d = (k * jnp.exp(mid - b)).astype(BF16)

        b_last = b[n - 1:n, :]
        q_in = (qs * jnp.exp(b)).astype(BF16)
        k_st = (k * jnp.exp(b_last - b)).astype(BF16)
        decay = jnp.exp(b_last)
        v = v_ref[s]
        sg = sg_ref[s]

        for h in range(N_HEADS):
            sl = slice(h * HEAD_K, (h + 1) * HEAD_K)
            sv = slice(h * HEAD_V, (h + 1) * HEAD_V)
            scores = jnp.where(m_d, _dot_nt(q_d[:, sl], k_d[:, sl]), 0.0)
            for x_l, m_l in zip(x_lv, m_lv):
                scores = jnp.where(m_l, _dot_nt(x_l[:, sl], x_l[:, sl]), scores)
            p = p_ref[s, h]
            o = _dot(scores.astype(BF16), v[:, sv]) + _dot_nt(q_in[:, sl], p.astype(BF16))
            p_ref[s, h] = p * decay[:, sl] + _dot_tn(v[:, sv], k_st[:, sl])
            o = o * lax.rsqrt(jnp.mean(o * o, -1, keepdims=True) + RMS_EPS) * ng
            o_ref[s, :, sv] = (o * sg[:, sv]).astype(BF16)

    @pl.when(c == pl.num_programs(1) - 1)
    def _():
        for s in range(n_seq):
            for h in range(N_HEADS):
                s_out_ref[s, h] = p_ref[s, h].T


def _hgrn_rec(qs, lf, k, v, sg, norm_g, s0, bsz, t):
    m, d = v.shape
    dk = qs.shape[1]
    chunk = min(t, HGRN_CHUNK)
    nc = t // chunk
    n_seq = min(bsz, HGRN_SEQS if nc > 1 else 2 * HGRN_SEQS)
    assert bsz % n_seq == 0 and t % chunk == 0
    blk = lambda bi, ci: (bi, ci, 0)
    st = lambda bi, ci: (bi, 0, 0, 0)
    seq3 = lambda a: a.reshape(bsz, t, a.shape[1])
    in_specs = [pl.BlockSpec((n_seq, chunk, dk), blk)] * 3 + [pl.BlockSpec((n_seq, chunk, d), blk)] * 2
    in_specs.append(_resident((1, HEAD_V)))
    args = [seq3(qs), seq3(lf), seq3(k), seq3(v), seq3(sg), norm_g.reshape(1, HEAD_V)]
    if s0 is not None:
        in_specs.append(pl.BlockSpec((n_seq, N_HEADS, HEAD_K, HEAD_V), st))
        args.append(s0)
    og, s_fin = pl.pallas_call(
        functools.partial(_hgrn_rec_kernel, chunk=chunk, n_seq=n_seq, has_init=s0 is not None),
        grid=(bsz // n_seq, nc),
        in_specs=in_specs,
        out_specs=[pl.BlockSpec((n_seq, chunk, d), blk),
                   pl.BlockSpec((n_seq, N_HEADS, HEAD_K, HEAD_V), st)],
        out_shape=[jax.ShapeDtypeStruct((bsz, t, d), BF16),
                   jax.ShapeDtypeStruct((bsz, N_HEADS, HEAD_K, HEAD_V), F32)],
        scratch_shapes=[pltpu.VMEM((n_seq, N_HEADS, HEAD_V, HEAD_K), F32)],
        compiler_params=_params(2),
        name="hgrn_rec",
    )(*args)
    return og.reshape(m, d), s_fin


def _proj_res_ln_kernel(a_ref, res_ref, w_ref, g_ref, b_ref, o_ref):
    y = _dot(a_ref[...], w_ref[...])
    o_ref[...] = _ln(ALPHA * res_ref[...] + y, g_ref[...], b_ref[...])


def _proj_res_ln(a, res, w, g, b):
    m, d = res.shape
    tm = _row_tile(m)
    row = lambda i: (i, 0)
    return pl.pallas_call(
        _proj_res_ln_kernel,
        grid=(m // tm,),
        in_specs=[pl.BlockSpec((tm, a.shape[1]), row), pl.BlockSpec((tm, d), row),
                  _resident(w.shape), _resident((1, d)), _resident((1, d))],
        out_specs=pl.BlockSpec((tm, d), row),
        out_shape=jax.ShapeDtypeStruct((m, d), F32),
        compiler_params=_params(1),
        name="proj_res_ln",
    )(a, res, w, g.reshape(1, d), b.reshape(1, d))


def _glu_kernel(x_ref, w_ref, b_ref, u_ref):
    d = u_ref.shape[1]
    xb = x_ref[...].astype(BF16)
    a = _dot(xb, w_ref[:, 0:d]) + b_ref[:, 0:d]
    gate = _dot(xb, w_ref[:, d:]) + b_ref[:, d:]
    u_ref[...] = a * jax.nn.sigmoid(gate)


def _glu(x, w, b):
    m, d = x.shape
    tm = _row_tile(m)
    row = lambda i: (i, 0)
    return pl.pallas_call(
        _glu_kernel,
        grid=(m // tm,),
        in_specs=[pl.BlockSpec((tm, d), row), _resident(w.shape), _resident((1, 2 * d))],
        out_specs=pl.BlockSpec((tm, d), row),
        out_shape=jax.ShapeDtypeStruct((m, d), F32),
        compiler_params=_params(1),
        name="conv_glu",
    )(x, w, b.reshape(1, 2 * d))


def _conv_rows(ext_ref, wdw_ref, r0, rb):
    pad = CONV_HIST - (CONV_W - 1)
    cols = []
    for lb in range(ext_ref.shape[1] // LANES):
        ls = slice(lb * LANES, (lb + 1) * LANES)
        win = ext_ref[pl.ds(r0, rb + CONV_HIST), ls]
        y = None
        for r in range(SUBLANES):
            rows = rb + (SUBLANES if r else 0)
            z = None
            for a in range(CONV_HIST // SUBLANES + 1):
                j = SUBLANES * a + r - pad
                if 0 <= j < CONV_W:
                    term = wdw_ref[j:j + 1, ls] * win[SUBLANES * a:SUBLANES * a + rows, :]
                    z = term if z is None else z + term
            z = z[r:r + rb, :] if r else z
            y = z if y is None else y + z
        cols.append(y)
    return jnp.concatenate(cols, axis=1)


def _conv_kernel(*refs, tt, rb, has_cache):
    if has_cache:
        (u_ref, cache_ref, x_ref, wdw_ref, bdw_ref, lng_ref, lnb_ref, w2_ref, b2_ref, g_ref, b_ref,
         o_ref, nc_ref, ext_ref, act_ref) = refs
    else:
        (u_ref, x_ref, wdw_ref, bdw_ref, lng_ref, lnb_ref, w2_ref, b2_ref, g_ref, b_ref,
         o_ref, nc_ref, ext_ref, act_ref) = refs
    ti = pl.program_id(1)
    d = u_ref.shape[1]
    pad = CONV_HIST - (CONV_W - 1)

    @pl.when(ti == 0)
    def _():
        ext_ref[0:CONV_HIST, :] = jnp.zeros((CONV_HIST, d), F32)
        if has_cache:
            ext_ref[pad:CONV_HIST, :] = cache_ref[0]

    @pl.when(ti > 0)
    def _():
        ext_ref[0:CONV_HIST, :] = ext_ref[tt:tt + CONV_HIST, :]

    ext_ref[CONV_HIST:CONV_HIST + tt, :] = u_ref[...]

    def block(i, carry):
        r0 = pl.multiple_of(i * rb, rb)
        y = _ln(_conv_rows(ext_ref, wdw_ref, r0, rb) + bdw_ref[...], lng_ref[...], lnb_ref[...])
        act_ref[pl.ds(r0, rb), :] = (y * jax.nn.sigmoid(y)).astype(BF16)
        return carry

    lax.fori_loop(0, tt // rb, block, 0)
    mix = _dot(act_ref[...], w2_ref[...]) + b2_ref[...]
    o_ref[...] = _ln(ALPHA * x_ref[...] + mix, g_ref[...], b_ref[...])

    @pl.when(ti == pl.num_programs(1) - 1)
    def _():
        nc_ref[0] = ext_ref[tt + pad:tt + CONV_HIST, :]


def _conv_mix(u, cache, x, wdw, bdw, lng, lnb, w2, b2, g, b, bsz, t):
    m, d = u.shape
    tt = min(t, 512)
    nt = t // tt
    rb = min(tt, CONV_ROWS)
    assert t % tt == 0 and tt % rb == 0
    row = lambda bi, ti: (bi * nt + ti, 0)
    vec = lambda a: a.reshape(1, d)
    in_specs = [pl.BlockSpec((tt, d), row)]
    args = [u]
    if cache is not None:
        in_specs.append(pl.BlockSpec((1, CONV_W - 1, d), lambda bi, ti: (bi, 0, 0)))
        args.append(cache)
    in_specs += [pl.BlockSpec((tt, d), row), _resident(wdw.shape), _resident((1, d)), _resident((1, d)),
                 _resident((1, d)), _resident(w2.shape), _resident((1, d)), _resident((1, d)), _resident((1, d))]
    args += [x, wdw, vec(bdw), vec(lng), vec(lnb), w2, vec(b2), vec(g), vec(b)]
    return pl.pallas_call(
        functools.partial(_conv_kernel, tt=tt, rb=rb, has_cache=cache is not None),
        grid=(bsz, nt),
        in_specs=in_specs,
        out_specs=[pl.BlockSpec((tt, d), row),
                   pl.BlockSpec((1, CONV_W - 1, d), lambda bi, ti: (bi, 0, 0))],
        out_shape=[jax.ShapeDtypeStruct((m, d), F32),
                   jax.ShapeDtypeStruct((bsz, CONV_W - 1, d), F32)],
        scratch_shapes=[pltpu.VMEM((CONV_HIST + tt, d), F32), pltpu.VMEM((tt, d), BF16)],
        compiler_params=_params(2),
        name="conv_mix",
    )(*args)


def _trunk(x3, hgrn_state, conv_cache, p, wb):
    bsz, t, d = x3.shape
    x = x3.reshape(bsz * t, d)
    ln = lambda i, s: (p['ln_g'][i, s], p['ln_b'][i, s])

    def ffn(x, i, s, casts):
        todo = [(name, src, lead) for name, src, lead in casts if name not in wb]
        y, cast = _ffn_ln(x, wb[f'g{i}{s}'], wb[f'u{i}{s}'], wb[f'd{i}{s}'], *ln(i, 2 * s),
                          casts=[(src, lead) for _, src, lead in todo])
        wb.update({name: c for (name, _, _), c in zip(todo, cast)})
        return y

    def ffn_casts(i, s):
        return [(f'g{i}{s}', p['ffn_w_gate'], (i, s)), (f'u{i}{s}', p['ffn_w_up'], (i, s)),
                (f'd{i}{s}', p['ffn_w_down'], (i, s))]

    x = ffn(x, 0, 0, ffn_casts(0, 1) + [('w_in', p['hgrn_w_in'], (0,)), ('w_out', p['hgrn_w_out'], (0,))])
    qs, lf, k, v, sg = _hgrn_in(x, wb['w_in'], p['hgrn_lb'], 0)
    og, h_new = _hgrn_rec(qs, lf, k, v, sg, p['hgrn_norm_g'][0], None if hgrn_state is None else hgrn_state[0], bsz, t)
    x = _proj_res_ln(og, x, wb['w_out'], *ln(0, 1))
    x = ffn(x, 0, 1, ffn_casts(1, 0) + [('pw1', p['conv_w_pw1'], (0,)), ('pw2', p['conv_w_pw2'], (0,))])
    x = ffn(x, 1, 0, ffn_casts(1, 1))
    u = _glu(x, wb['pw1'], p['conv_b_pw1'][0])
    x, c_new = _conv_mix(u, None if conv_cache is None else conv_cache[0], x, p['conv_w_dw'][0], p['conv_b_dw'][0],
                         p['conv_ln_g'][0], p['conv_ln_b'][0], wb['pw2'], p['conv_b_pw2'][0], *ln(1, 1), bsz, t)
    x = ffn(x, 1, 1, [])
    return x.reshape(bsz, t, d), h_new[None], c_new[None]


def kernel(x_prompt, x_sample, state_hgrn, cache_conv, ffn_w_gate, ffn_w_up, ffn_w_down, ln_g, ln_b, hgrn_w_in, hgrn_lb, hgrn_norm_g, hgrn_w_out, conv_w_pw1, conv_b_pw1, conv_w_dw, conv_b_dw, conv_ln_g, conv_ln_b, conv_w_pw2, conv_b_pw2):
    assert ffn_w_gate.shape[0] == DEPTH == 2 and hgrn_w_in.shape[0] == 1 and conv_w_pw1.shape[0] == 1
    p = {'ffn_w_gate': ffn_w_gate, 'ffn_w_up': ffn_w_up, 'ffn_w_down': ffn_w_down, 'ln_g': ln_g, 'ln_b': ln_b,
         'hgrn_w_in': hgrn_w_in, 'hgrn_lb': hgrn_lb, 'hgrn_norm_g': hgrn_norm_g, 'hgrn_w_out': hgrn_w_out,
         'conv_w_pw1': conv_w_pw1, 'conv_b_pw1': conv_b_pw1, 'conv_w_dw': conv_w_dw, 'conv_b_dw': conv_b_dw,
         'conv_ln_g': conv_ln_g, 'conv_ln_b': conv_ln_b, 'conv_w_pw2': conv_w_pw2, 'conv_b_pw2': conv_b_pw2}
    wb = {'g00': ffn_w_gate[0, 0].astype(BF16), 'u00': ffn_w_up[0, 0].astype(BF16), 'd00': ffn_w_down[0, 0].astype(BF16)}
    y_p, h_p, c_p = _trunk(x_prompt, None, None, p, wb)
    y_s, h_s, c_s = _trunk(x_sample, state_hgrn, cache_conv, p, wb)
    return (y_p, y_s, h_p, h_s, c_p, c_s)
```

```python
import functools
import math

import jax
import jax.numpy as jnp
from jax import lax
from jax.experimental import pallas as pl
from jax.experimental.pallas import tpu as pltpu

F32 = jnp.float32
BF16 = jnp.bfloat16

DEPTH = 2
N_HEADS = 8
HEAD_K = 128
HEAD_V = 128
CONV_W = 31
HGRN_BLOCK = 16
ALPHA = (2 * DEPTH) ** 0.25
LN_EPS = 1e-5
RMS_EPS = 1e-5
LOG2E = math.log2(math.e)

V7X_VMEM_LIMIT_BYTES = 56 * 1024 * 1024
LANES = 128
SUBLANES = 8
BF16_ROWS = 16
CONV_HIST = 32
CONV_ROWS = 64
HGRN_CHUNK = 128
HGRN_SEQS = 4


def _row_tile(m):
    return 512 if m % 512 == 0 else m


def _resident(shape):
    return pl.BlockSpec(shape, lambda *_: (0,) * len(shape), pipeline_mode=pl.Buffered(1))


def _params(n_axes):
    return pltpu.CompilerParams(
        dimension_semantics=("arbitrary",) * n_axes,
        vmem_limit_bytes=V7X_VMEM_LIMIT_BYTES)


def _ln(y, g, b):
    mu = jnp.mean(y, -1, keepdims=True)
    d = y - mu
    var = jnp.mean(d * d, -1, keepdims=True)
    return d * lax.rsqrt(var + LN_EPS) * g + b


def _silu(x):
    return x * jax.nn.sigmoid(x)


def _dot(a, b):
    return jnp.dot(a, b, preferred_element_type=F32)


def _dot_nt(a, b):
    return lax.dot_general(a, b, (((1,), (1,)), ((), ())), preferred_element_type=F32)


def _dot_tn(a, b):
    return lax.dot_general(a, b, (((0,), (0,)), ((), ())), preferred_element_type=F32)


def _cast_specs(w, lead, steps):
    rows, cols = w.shape[-2:]
    nblk = min(steps, rows // BF16_ROWS)
    rb = rows // nblk
    assert rb * nblk == rows and rb % BF16_ROWS == 0
    blk = lambda i: jnp.minimum(i, nblk - 1)
    in_spec = pl.BlockSpec((None,) * len(lead) + (rb, cols), lambda i: tuple(lead) + (blk(i), 0))
    out_spec = pl.BlockSpec((rb, cols), lambda i: (blk(i), 0))
    return in_spec, out_spec, jax.ShapeDtypeStruct((rows, cols), BF16)


def _ffn_ln_kernel(*refs, fc, n_cast):
    x_ref, wg_ref, wu_ref, wd_ref, g_ref, b_ref = refs[:6]
    cast_in = refs[6:6 + n_cast]
    o_ref = refs[6 + n_cast]
    cast_out = refs[7 + n_cast:7 + 2 * n_cast]
    h_ref = refs[7 + 2 * n_cast]
    x = x_ref[...]
    xb = x.astype(BF16)
    d_ff = wg_ref.shape[1]
    for c in range(d_ff // fc):
        sl = slice(c * fc, (c + 1) * fc)
        h_ref[:, sl] = (_silu(_dot(xb, wg_ref[:, sl])) * _dot(xb, wu_ref[:, sl])).astype(BF16)
    y = _dot(h_ref[...], wd_ref[...])
    o_ref[...] = _ln(ALPHA * x + 0.5 * y, g_ref[...], b_ref[...])
    for src, dst in zip(cast_in, cast_out):
        dst[...] = src[...].astype(BF16)


def _ffn_ln(x, wg, wu, wd, g, b, casts=()):
    m, d = x.shape
    d_ff = wg.shape[1]
    tm = _row_tile(m)
    steps = m // tm
    row = lambda i: (i, 0)
    cast_specs = [_cast_specs(w, lead, steps) for w, lead in casts]
    outs = pl.pallas_call(
        functools.partial(_ffn_ln_kernel, fc=512, n_cast=len(casts)),
        grid=(steps,),
        in_specs=[pl.BlockSpec((tm, d), row),
                  _resident((d, d_ff)), _resident((d, d_ff)), _resident((d_ff, d)),
                  _resident((1, d)), _resident((1, d))] + [s[0] for s in cast_specs],
        out_specs=[pl.BlockSpec((tm, d), row)] + [s[1] for s in cast_specs],
        out_shape=[jax.ShapeDtypeStruct((m, d), F32)] + [s[2] for s in cast_specs],
        scratch_shapes=[pltpu.VMEM((tm, d_ff), BF16)],
        compiler_params=_params(1),
        name="ffn_ln",
    )(x, wg, wu, wd, g.reshape(1, d), b.reshape(1, d), *[w for w, _ in casts])
    return outs[0], outs[1:]


def _hgrn_operands(qs, k, b2, n):
    levels = []
    l = n // 2
    while l >= HGRN_BLOCK:
        parts = []
        for j in range(n // l):
            r = slice(j * l, (j + 1) * l)
            a = (j | 1) * l - 1
            anchor = b2[a:a + 1, :]
            parts.append(qs[r] * jnp.exp2(b2[r] - anchor) if j & 1 else k[r] * jnp.exp2(anchor - b2[r]))
        levels.append((l, jnp.concatenate(parts, axis=0).astype(BF16)))
        l //= 2
    qd, kd = [], []
    for j in range(n // HGRN_BLOCK):
        r = slice(j * HGRN_BLOCK, (j + 1) * HGRN_BLOCK)
        m = j * HGRN_BLOCK + HGRN_BLOCK // 2
        rel = b2[r] - b2[m:m + 1, :]
        qd.append(qs[r] * jnp.exp2(rel))
        kd.append(k[r] * jnp.exp2(-rel))
    cat = lambda ps: (ps[0] if len(ps) == 1 else jnp.concatenate(ps, axis=0)).astype(BF16)
    b_last = b2[n - 1:n, :]
    q_in = (qs * jnp.exp2(b2)).astype(BF16)
    k_st = (k * jnp.exp2(b_last - b2)).astype(BF16)
    return levels, cat(qd), cat(kd), q_in, k_st, jnp.exp2(b_last)


def _hgrn_mixer_kernel(*refs, chunk, n_seq, layer, has_init):
    if has_init:
        (x_ref, win_ref, lb_ref, tri_ref, ng_ref, wout_ref, g_ref, b_ref, s0_ref, o_ref, s_out_ref,
         p_ref, qs_s, lf_s, k_s, v_s, sg_s, og_s) = refs
    else:
        (x_ref, win_ref, lb_ref, tri_ref, ng_ref, wout_ref, g_ref, b_ref, o_ref, s_out_ref,
         p_ref, qs_s, lf_s, k_s, v_s, sg_s, og_s) = refs
        s0_ref = None
    c = pl.program_id(1)
    n = chunk
    rows = n_seq * n
    d = x_ref.shape[-1]
    dk = N_HEADS * HEAD_K

    @pl.when(c == 0)
    def _():
        for s in range(n_seq):
            for h in range(N_HEADS):
                p_ref[s, h] = s0_ref[s, h].T if has_init else jnp.zeros((HEAD_V, HEAD_K), F32)

    xb = x_ref[...].reshape(rows, d).astype(BF16)
    lbp = lb_ref[...]
    e = jnp.exp(lbp - jnp.max(lbp, axis=0, keepdims=True))
    lb = jnp.sum(e[:layer + 1], axis=0, keepdims=True) / jnp.sum(e, axis=0, keepdims=True)
    qs_s[...] = _silu(_dot(xb, win_ref[:, 0:dk])) * (HEAD_K ** -0.5)
    forget = lb + (1.0 - lb) * jax.nn.sigmoid(_dot(xb, win_ref[:, dk:2 * dk]))
    k_s[...] = 1.0 - forget
    lf_s[...] = jnp.log(forget)
    v_s[...] = _dot(xb, win_ref[:, 2 * dk:2 * dk + d]).astype(BF16)
    sg_s[...] = _silu(_dot(xb, win_ref[:, 2 * dk + d:])).astype(BF16)

    t_idx = lax.broadcasted_iota(jnp.int32, (n, n), 0)
    s_idx = lax.broadcasted_iota(jnp.int32, (n, n), 1)
    m_d = ((t_idx // HGRN_BLOCK) == (s_idx // HGRN_BLOCK)) & (s_idx <= t_idx)
    masks = {}
    l = n // 2
    while l >= HGRN_BLOCK:
        masks[l] = (((t_idx // l) & 1) == 1) & ((s_idx // l) == (t_idx // l) - 1)
        l //= 2
    ng = ng_ref[...]
    tri = tri_ref[...]
    heads = [(slice(h * HEAD_K, (h + 1) * HEAD_K), slice(h * HEAD_V, (h + 1) * HEAD_V)) for h in range(N_HEADS)]

    for s in range(n_seq):
        r = slice(s * n, (s + 1) * n)
        lf = lf_s[r, :]
        lf_hi = lf.astype(BF16)
        r1 = lf - lf_hi.astype(F32)
        lf_mid = r1.astype(BF16)
        lf_lo = (r1 - lf_mid.astype(F32)).astype(BF16)
        b2 = (_dot(tri, lf_hi) + _dot(tri, lf_mid) + _dot(tri, lf_lo)) * LOG2E
        levels, q_d, k_d, q_in, k_st, decay = _hgrn_operands(qs_s[r, :], k_s[r, :], b2, n)
        v = v_s[r, :]

        scores = []
        for sl, _ in heads:
            sc = jnp.where(m_d, _dot_nt(q_d[:, sl], k_d[:, sl]), 0.0)
            for l, x_l in levels:
                sc = jnp.where(masks[l], _dot_nt(x_l[:, sl], x_l[:, sl]), sc)
            scores.append(sc.astype(BF16))
        outs = [_dot(sc, v[:, sv]) + _dot_nt(q_in[:, sl], p_ref[s, h].astype(BF16))
                for h, ((sl, sv), sc) in enumerate(zip(heads, scores))]
        for h, (sl, sv) in enumerate(heads):
            p_ref[s, h] = p_ref[s, h] * decay[:, sl] + _dot_tn(v[:, sv], k_st[:, sl])
        for (sl, sv), o in zip(heads, outs):
            o = o * lax.rsqrt(jnp.mean(o * o, -1, keepdims=True) + RMS_EPS) * ng
            og_s[r, sv] = (o * sg_s[r, sv]).astype(BF16)

    y = _dot(og_s[...], wout_ref[...])
    o_ref[...] = _ln(ALPHA * x_ref[...].reshape(rows, d) + y, g_ref[...], b_ref[...]).reshape(n_seq, n, d)

    @pl.when(c == pl.num_programs(1) - 1)
    def _():
        for s in range(n_seq):
            for h in range(N_HEADS):
                s_out_ref[s, h] = p_ref[s, h].T


def _hgrn_mixer(x, w_in, lb_param, norm_g, w_out, g, b, s0, bsz, t, layer):
    m, d = x.shape
    dk = N_HEADS * HEAD_K
    chunk = min(t, HGRN_CHUNK)
    nc = t // chunk
    n_seq = min(bsz, HGRN_SEQS if nc > 1 else 2 * HGRN_SEQS)
    assert bsz % n_seq == 0 and t % chunk == 0
    rows = n_seq * chunk
    blk = lambda bi, ci: (bi, ci, 0)
    st = lambda bi, ci: (bi, 0, 0, 0)
    tri = jnp.tril(jnp.ones((chunk, chunk), BF16))
    in_specs = [pl.BlockSpec((n_seq, chunk, d), blk), _resident(w_in.shape), _resident(lb_param.shape),
                _resident(tri.shape), _resident((1, HEAD_V)), _resident(w_out.shape),
                _resident((1, d)), _resident((1, d))]
    args = [x.reshape(bsz, t, d), w_in, lb_param, tri, norm_g.reshape(1, HEAD_V), w_out,
            g.reshape(1, d), b.reshape(1, d)]
    if s0 is not None:
        in_specs.append(pl.BlockSpec((n_seq, N_HEADS, HEAD_K, HEAD_V), st))
        args.append(s0)
    y, s_fin = pl.pallas_call(
        functools.partial(_hgrn_mixer_kernel, chunk=chunk, n_seq=n_seq, layer=layer, has_init=s0 is not None),
        grid=(bsz // n_seq, nc),
        in_specs=in_specs,
        out_specs=[pl.BlockSpec((n_seq, chunk, d), blk),
                   pl.BlockSpec((n_seq, N_HEADS, HEAD_K, HEAD_V), st)],
        out_shape=[jax.ShapeDtypeStruct((bsz, t, d), F32),
                   jax.ShapeDtypeStruct((bsz, N_HEADS, HEAD_K, HEAD_V), F32)],
        scratch_shapes=[pltpu.VMEM((n_seq, N_HEADS, HEAD_V, HEAD_K), F32),
                        pltpu.VMEM((rows, dk), F32), pltpu.VMEM((rows, dk), F32), pltpu.VMEM((rows, dk), F32),
                        pltpu.VMEM((rows, d), BF16), pltpu.VMEM((rows, d), BF16), pltpu.VMEM((rows, d), BF16)],
        compiler_params=_params(2),
        name="hgrn_mixer",
    )(*args)
    return y.reshape(m, d), s_fin


def _conv_rows(ext_ref, wdw_ref, r0, rb):
    pad = CONV_HIST - (CONV_W - 1)
    cols = []
    for lb in range(ext_ref.shape[1] // LANES):
        ls = slice(lb * LANES, (lb + 1) * LANES)
        win = ext_ref[pl.ds(r0, rb + CONV_HIST), ls]
        y = None
        for r in range(SUBLANES):
            rows = rb + (SUBLANES if r else 0)
            z = None
            for a in range(CONV_HIST // SUBLANES + 1):
                j = SUBLANES * a + r - pad
                if 0 <= j < CONV_W:
                    term = wdw_ref[j:j + 1, ls] * win[SUBLANES * a:SUBLANES * a + rows, :]
                    z = term if z is None else z + term
            z = z[r:r + rb, :] if r else z
            y = z if y is None else y + z
        cols.append(y)
    return jnp.concatenate(cols, axis=1)


def _conv_mixer_kernel(*refs, tt, rb, has_cache):
    if has_cache:
        (x_ref, cache_ref, w1_ref, b1_ref, wdw_ref, bdw_ref, lng_ref, lnb_ref, w2_ref, b2_ref, g_ref, b_ref,
         o_ref, nc_ref, ext_ref, act_ref) = refs
    else:
        (x_ref, w1_ref, b1_ref, wdw_ref, bdw_ref, lng_ref, lnb_ref, w2_ref, b2_ref, g_ref, b_ref,
         o_ref, nc_ref, ext_ref, act_ref) = refs
    ti = pl.program_id(1)
    d = x_ref.shape[1]
    pad = CONV_HIST - (CONV_W - 1)

    @pl.when(ti == 0)
    def _():
        ext_ref[0:CONV_HIST, :] = jnp.zeros((CONV_HIST, d), F32)
        if has_cache:
            ext_ref[pad:CONV_HIST, :] = cache_ref[0]

    @pl.when(ti > 0)
    def _():
        ext_ref[0:CONV_HIST, :] = ext_ref[tt:tt + CONV_HIST, :]

    xb = x_ref[...].astype(BF16)
    a = _dot(xb, w1_ref[:, 0:d]) + b1_ref[:, 0:d]
    gate = _dot(xb, w1_ref[:, d:]) + b1_ref[:, d:]
    ext_ref[CONV_HIST:CONV_HIST + tt, :] = a * jax.nn.sigmoid(gate)

    def block(i, carry):
        r0 = pl.multiple_of(i * rb, rb)
        y = _ln(_conv_rows(ext_ref, wdw_ref, r0, rb) + bdw_ref[...], lng_ref[...], lnb_ref[...])
        act_ref[pl.ds(r0, rb), :] = _silu(y).astype(BF16)
        return carry

    lax.fori_loop(0, tt // rb, block, 0)
    mix = _dot(act_ref[...], w2_ref[...]) + b2_ref[...]
    o_ref[...] = _ln(ALPHA * x_ref[...] + mix, g_ref[...], b_ref[...])

    @pl.when(ti == pl.num_programs(1) - 1)
    def _():
        nc_ref[0] = ext_ref[tt + pad:tt + CONV_HIST, :]


def _conv_mixer(x, cache, w1, b1, wdw, bdw, lng, lnb, w2, b2, g, b, bsz, t):
    m, d = x.shape
    tt = min(t, 512)
    nt = t // tt
    rb = min(tt, CONV_ROWS)
    assert t % tt == 0 and tt % rb == 0
    row = lambda bi, ti: (bi * nt + ti, 0)
    vec = lambda a: a.reshape(1, -1)
    in_specs = [pl.BlockSpec((tt, d), row)]
    args = [x]
    if cache is not None:
        in_specs.append(pl.BlockSpec((1, CONV_W - 1, d), lambda bi, ti: (bi, 0, 0)))
        args.append(cache)
    consts = [w1, vec(b1), wdw, vec(bdw), vec(lng), vec(lnb), w2, vec(b2), vec(g), vec(b)]
    in_specs += [_resident(a.shape) for a in consts]
    return pl.pallas_call(
        functools.partial(_conv_mixer_kernel, tt=tt, rb=rb, has_cache=cache is not None),
        grid=(bsz, nt),
        in_specs=in_specs,
        out_specs=[pl.BlockSpec((tt, d), row),
                   pl.BlockSpec((1, CONV_W - 1, d), lambda bi, ti: (bi, 0, 0))],
        out_shape=[jax.ShapeDtypeStruct((m, d), F32),
                   jax.ShapeDtypeStruct((bsz, CONV_W - 1, d), F32)],
        scratch_shapes=[pltpu.VMEM((CONV_HIST + tt, d), F32), pltpu.VMEM((tt, d), BF16)],
        compiler_params=_params(2),
        name="conv_mixer",
    )(*args, *consts)


def _trunk(x3, hgrn_state, conv_cache, p, wb):
    bsz, t, d = x3.shape
    x = x3.reshape(bsz * t, d)
    ln = lambda i, s: (p['ln_g'][i, s], p['ln_b'][i, s])

    def ffn(x, i, s, casts):
        todo = [(name, src, lead) for name, src, lead in casts if name not in wb]
        y, cast = _ffn_ln(x, wb[f'g{i}{s}'], wb[f'u{i}{s}'], wb[f'd{i}{s}'], *ln(i, 2 * s),
                          casts=[(src, lead) for _, src, lead in todo])
        wb.update({name: c for (name, _, _), c in zip(todo, cast)})
        return y

    def ffn_casts(i, s):
        return [(f'g{i}{s}', p['ffn_w_gate'], (i, s)), (f'u{i}{s}', p['ffn_w_up'], (i, s)),
                (f'd{i}{s}', p['ffn_w_down'], (i, s))]

    x = ffn(x, 0, 0, ffn_casts(0, 1) + [('w_in', p['hgrn_w_in'], (0,)), ('w_out', p['hgrn_w_out'], (0,))])
    x, h_new = _hgrn_mixer(x, wb['w_in'], p['hgrn_lb'], p['hgrn_norm_g'][0], wb['w_out'], *ln(0, 1),
                           None if hgrn_state is None else hgrn_state[0], bsz, t, 0)
    x = ffn(x, 0, 1, ffn_casts(1, 0) + [('pw1', p['conv_w_pw1'], (0,)), ('pw2', p['conv_w_pw2'], (0,))])
    x = ffn(x, 1, 0, ffn_casts(1, 1))
    x, c_new = _conv_mixer(x, None if conv_cache is None else conv_cache[0], wb['pw1'], p['conv_b_pw1'][0],
                           p['conv_w_dw'][0], p['conv_b_dw'][0], p['conv_ln_g'][0], p['conv_ln_b'][0],
                           wb['pw2'], p['conv_b_pw2'][0], *ln(1, 1), bsz, t)
    x = ffn(x, 1, 1, [])
    return x.reshape(bsz, t, d), h_new[None], c_new[None]


def kernel(x_prompt, x_sample, state_hgrn, cache_conv, ffn_w_gate, ffn_w_up, ffn_w_down, ln_g, ln_b, hgrn_w_in, hgrn_lb, hgrn_norm_g, hgrn_w_out, conv_w_pw1, conv_b_pw1, conv_w_dw, conv_b_dw, conv_ln_g, conv_ln_b, conv_w_pw2, conv_b_pw2):
    assert ffn_w_gate.shape[0] == DEPTH == 2 and hgrn_w_in.shape[0] == 1 and conv_w_pw1.shape[0] == 1
    p = {'ffn_w_gate': ffn_w_gate, 'ffn_w_up': ffn_w_up, 'ffn_w_down': ffn_w_down, 'ln_g': ln_g, 'ln_b': ln_b,
         'hgrn_w_in': hgrn_w_in, 'hgrn_lb': hgrn_lb, 'hgrn_norm_g': hgrn_norm_g, 'hgrn_w_out': hgrn_w_out,
         'conv_w_pw1': conv_w_pw1, 'conv_b_pw1': conv_b_pw1, 'conv_w_dw': conv_w_dw, 'conv_b_dw': conv_b_dw,
         'conv_ln_g': conv_ln_g, 'conv_ln_b': conv_ln_b, 'conv_w_pw2': conv_w_pw2, 'conv_b_pw2': conv_b_pw2}
    wb = {'g00': ffn_w_gate[0, 0].astype(BF16), 'u00': ffn_w_up[0, 0].astype(BF16), 'd00': ffn_w_down[0, 0].astype(BF16)}
    y_p, h_p, c_p = _trunk(x_prompt, None, None, p, wb)
    y_s, h_s, c_s = _trunk(x_sample, state_hgrn, cache_conv, p, wb)
    return (y_p, y_s, h_p, h_s, c_p, c_s)
```

```python
import functools
import math

import jax
import jax.numpy as jnp
from jax import lax
from jax.experimental import pallas as pl
from jax.experimental.pallas import tpu as pltpu

F32 = jnp.float32
BF16 = jnp.bfloat16

DEPTH = 2
N_HEADS = 8
HEAD_K = 128
HEAD_V = 128
CONV_W = 31
HGRN_BLOCK = 16
ALPHA = (2 * DEPTH) ** 0.25
LN_EPS = 1e-5
RMS_EPS = 1e-5
LOG2E = math.log2(math.e)

V7X_VMEM_LIMIT_BYTES = 56 * 1024 * 1024
LANES = 128
SUBLANES = 8
BF16_ROWS = 16
CONV_HIST = 32
CONV_ROWS = 128
FFN_ROW_SPLIT = 2
HGRN_GROUP = 2
FFN_SMALL_COLS = 512
FFN_SMALL_ROWS = 256
CONV_SEQS = 16
HGRN_CHUNK = 128
HGRN_SEQS = 4


def _row_tile(m):
    return 512 if m % 512 == 0 else m


def _resident(shape):
    return pl.BlockSpec(shape, lambda *_: (0,) * len(shape), pipeline_mode=pl.Buffered(1))


def _params(n_axes):
    return pltpu.CompilerParams(
        dimension_semantics=("arbitrary",) * n_axes,
        vmem_limit_bytes=V7X_VMEM_LIMIT_BYTES)


def _ln(y, g, b):
    mu = jnp.mean(y, -1, keepdims=True)
    d = y - mu
    var = jnp.mean(d * d, -1, keepdims=True)
    return d * lax.rsqrt(var + LN_EPS) * g + b


def _silu(x):
    return x * jax.nn.sigmoid(x)


def _dot(a, b):
    return jnp.dot(a, b, preferred_element_type=F32)


def _dot_nt(a, b):
    return lax.dot_general(a, b, (((1,), (1,)), ((), ())), preferred_element_type=F32)


def _dot_tn(a, b):
    return lax.dot_general(a, b, (((0,), (0,)), ((), ())), preferred_element_type=F32)


def _cast_specs(w, lead, steps):
    rows, cols = w.shape[-2:]
    nblk = min(steps, rows // BF16_ROWS)
    rb = rows // nblk
    assert rb * nblk == rows and rb % BF16_ROWS == 0
    blk = lambda i: jnp.minimum(i, nblk - 1)
    in_spec = pl.BlockSpec((None,) * len(lead) + (rb, cols), lambda i: tuple(lead) + (blk(i), 0))
    out_spec = pl.BlockSpec((rb, cols), lambda i: (blk(i), 0))
    return in_spec, out_spec, jax.ShapeDtypeStruct((rows, cols), BF16)


def _ffn_ln_kernel(*refs, fc, n_cast):
    x_ref, wg_ref, wu_ref, wd_ref, g_ref, b_ref = refs[:6]
    cast_in = refs[6:6 + n_cast]
    o_ref = refs[6 + n_cast]
    cast_out = refs[7 + n_cast:7 + 2 * n_cast]
    h_ref = refs[7 + 2 * n_cast]
    d_ff = wg_ref.shape[1]
    half = x_ref.shape[0] // FFN_ROW_SPLIT
    for hb in range(FFN_ROW_SPLIT):
        rs = slice(hb * half, (hb + 1) * half)
        x = x_ref[rs, :]
        xb = x.astype(BF16)
        for c in range(d_ff // fc):
            sl = slice(c * fc, (c + 1) * fc)
            h_ref[rs, sl] = (_silu(_dot(xb, wg_ref[:, sl])) * _dot(xb, wu_ref[:, sl])).astype(BF16)
        y = _dot(h_ref[rs, :], wd_ref[...])
        o_ref[rs, :] = _ln(ALPHA * x + 0.5 * y, g_ref[...], b_ref[...])
    for src, dst in zip(cast_in, cast_out):
        dst[...] = src[...].astype(BF16)


def _ffn_ln(x, wg, wu, wd, g, b, casts=()):
    m, d = x.shape
    d_ff = wg.shape[1]
    tm = _row_tile(m)
    steps = m // tm
    row = lambda i: (i, 0)
    cast_specs = [_cast_specs(w, lead, steps) for w, lead in casts]
    outs = pl.pallas_call(
        functools.partial(_ffn_ln_kernel, fc=512, n_cast=len(casts)),
        grid=(steps,),
        in_specs=[pl.BlockSpec((tm, d), row),
                  _resident((d, d_ff)), _resident((d, d_ff)), _resident((d_ff, d)),
                  _resident((1, d)), _resident((1, d))] + [s[0] for s in cast_specs],
        out_specs=[pl.BlockSpec((tm, d), row)] + [s[1] for s in cast_specs],
        out_shape=[jax.ShapeDtypeStruct((m, d), F32)] + [s[2] for s in cast_specs],
        scratch_shapes=[pltpu.VMEM((tm, d_ff), BF16)],
        compiler_params=_params(1),
        name="ffn_ln",
    )(x, wg, wu, wd, g.reshape(1, d), b.reshape(1, d), *[w for w, _ in casts])
    return outs[0], outs[1:]


def _ffn_small_kernel(*refs, emit):
    x_ref, wg_ref, wu_ref, wd_ref, g_ref, b_ref, o_ref = refs[:7]
    acc_ref = refs[-1]
    c = pl.program_id(0)
    wg, wu, wd = wg_ref[...].astype(BF16), wu_ref[...].astype(BF16), wd_ref[...].astype(BF16)
    if emit:
        for dst, w in zip(refs[7:10], (wg, wu, wd)):
            dst[...] = w
    x = x_ref[...]
    xb = x.astype(BF16)
    part = _dot((_silu(_dot(xb, wg)) * _dot(xb, wu)).astype(BF16), wd)

    @pl.when(c == 0)
    def _():
        acc_ref[...] = part

    @pl.when(c > 0)
    def _():
        acc_ref[...] += part

    @pl.when(c == pl.num_programs(0) - 1)
    def _():
        o_ref[...] = _ln(ALPHA * x + 0.5 * acc_ref[...], g_ref[...], b_ref[...])


def _ffn_ln_small(x, wg, wu, wd, lead, g, b):
    m, d = x.shape
    d_ff = wg.shape[-1]
    fc = FFN_SMALL_COLS
    emit = wg.dtype != BF16
    sq = (None,) * len(lead)
    col = lambda c: tuple(lead) + (0, c)
    row = lambda c: tuple(lead) + (c, 0)
    out_specs = [pl.BlockSpec((m, d), lambda c: (0, 0))]
    out_shape = [jax.ShapeDtypeStruct((m, d), F32)]
    if emit:
        out_specs += [pl.BlockSpec((d, fc), lambda c: (0, c)), pl.BlockSpec((d, fc), lambda c: (0, c)),
                      pl.BlockSpec((fc, d), lambda c: (c, 0))]
        out_shape += [jax.ShapeDtypeStruct((d, d_ff), BF16), jax.ShapeDtypeStruct((d, d_ff), BF16),
                      jax.ShapeDtypeStruct((d_ff, d), BF16)]
    outs = pl.pallas_call(
        functools.partial(_ffn_small_kernel, emit=emit),
        grid=(d_ff // fc,),
        in_specs=[_resident((m, d)), pl.BlockSpec(sq + (d, fc), col), pl.BlockSpec(sq + (d, fc), col),
                  pl.BlockSpec(sq + (fc, d), row), _resident((1, d)), _resident((1, d))],
        out_specs=out_specs,
        out_shape=out_shape,
        scratch_shapes=[pltpu.VMEM((m, d), F32)],
        compiler_params=_params(1),
        name="ffn_ln_small",
    )(x, wg, wu, wd, g.reshape(1, d), b.reshape(1, d))
    return outs[0], outs[1:]


def _hgrn_operands(qs, k, b2, n):
    levels = []
    l = n // 2
    while l >= HGRN_BLOCK:
        parts = []
        for j in range(n // l):
            r = slice(j * l, (j + 1) * l)
            a = (j | 1) * l - 1
            anchor = b2[a:a + 1, :]
            parts.append(qs[r] * jnp.exp2(b2[r] - anchor) if j & 1 else k[r] * jnp.exp2(anchor - b2[r]))
        levels.append((l, jnp.concatenate(parts, axis=0).astype(BF16)))
        l //= 2
    qd, kd = [], []
    for j in range(n // HGRN_BLOCK):
        r = slice(j * HGRN_BLOCK, (j + 1) * HGRN_BLOCK)
        m = j * HGRN_BLOCK + HGRN_BLOCK // 2
        rel = b2[r] - b2[m:m + 1, :]
        qd.append(qs[r] * jnp.exp2(rel))
        kd.append(k[r] * jnp.exp2(-rel))
    cat = lambda ps: (ps[0] if len(ps) == 1 else jnp.concatenate(ps, axis=0)).astype(BF16)
    b_last = b2[n - 1:n, :]
    q_in = (qs * jnp.exp2(b2)).astype(BF16)
    k_st = (k * jnp.exp2(b_last - b2)).astype(BF16)
    return levels, cat(qd), cat(kd), q_in, k_st, jnp.exp2(b_last)


def _hgrn_mixer_kernel(*refs, chunk, n_seq, layer, has_init):
    if has_init:
        (x_ref, win_ref, lb_ref, tri_ref, ng_ref, wout_ref, g_ref, b_ref, s0_ref, o_ref, s_out_ref,
         p_ref, qs_s, lf_s, k_s, v_s, sg_s, og_s) = refs
    else:
        (x_ref, win_ref, lb_ref, tri_ref, ng_ref, wout_ref, g_ref, b_ref, o_ref, s_out_ref,
         p_ref, qs_s, lf_s, k_s, v_s, sg_s, og_s) = refs
        s0_ref = None
    c = pl.program_id(1)
    n = chunk
    rows = n_seq * n
    d = x_ref.shape[-1]
    dk = N_HEADS * HEAD_K

    @pl.when(c == 0)
    def _():
        for s in range(n_seq):
            for h in range(N_HEADS):
                p_ref[s, h] = s0_ref[s, h].T if has_init else jnp.zeros((HEAD_V, HEAD_K), F32)

    xb = x_ref[...].reshape(rows, d).astype(BF16)
    lbp = lb_ref[...]
    e = jnp.exp(lbp - jnp.max(lbp, axis=0, keepdims=True))
    lb = jnp.sum(e[:layer + 1], axis=0, keepdims=True) / jnp.sum(e, axis=0, keepdims=True)
    qs_s[...] = _silu(_dot(xb, win_ref[:, 0:dk])) * (HEAD_K ** -0.5)
    forget = lb + (1.0 - lb) * jax.nn.sigmoid(_dot(xb, win_ref[:, dk:2 * dk]))
    k_s[...] = 1.0 - forget
    lf_s[...] = jnp.log(forget)
    v_s[...] = _dot(xb, win_ref[:, 2 * dk:2 * dk + d]).astype(BF16)
    sg_s[...] = _silu(_dot(xb, win_ref[:, 2 * dk + d:])).astype(BF16)

    t_idx = lax.broadcasted_iota(jnp.int32, (n, n), 0)
    s_idx = lax.broadcasted_iota(jnp.int32, (n, n), 1)
    m_d = ((t_idx // HGRN_BLOCK) == (s_idx // HGRN_BLOCK)) & (s_idx <= t_idx)
    masks = {}
    l = n // 2
    while l >= HGRN_BLOCK:
        masks[l] = (((t_idx // l) & 1) == 1) & ((s_idx // l) == (t_idx // l) - 1)
        l //= 2
    ng = ng_ref[...]
    tri = tri_ref[...]
    heads = [(slice(h * HEAD_K, (h + 1) * HEAD_K), slice(h * HEAD_V, (h + 1) * HEAD_V)) for h in range(N_HEADS)]

    def recur(seqs):
        ops = {}
        for s in seqs:
            r = slice(s * n, (s + 1) * n)
            lf = lf_s[r, :]
            lf_hi = lf.astype(BF16)
            r1 = lf - lf_hi.astype(F32)
            lf_mid = r1.astype(BF16)
            lf_lo = (r1 - lf_mid.astype(F32)).astype(BF16)
            b2 = (_dot(tri, lf_hi) + _dot(tri, lf_mid) + _dot(tri, lf_lo)) * LOG2E
            ops[s] = _hgrn_operands(qs_s[r, :], k_s[r, :], b2, n) + (v_s[r, :],)
        units = [(s, h, sl, sv) for s in seqs for h, (sl, sv) in enumerate(heads)]
        scores = []
        for s, h, sl, sv in units:
            levels, q_d, k_d = ops[s][:3]
            sc = jnp.where(m_d, _dot_nt(q_d[:, sl], k_d[:, sl]), 0.0)
            for l, x_l in levels:
                sc = jnp.where(masks[l], _dot_nt(x_l[:, sl], x_l[:, sl]), sc)
            scores.append(sc.astype(BF16))
        outs = [_dot(sc, ops[s][6][:, sv]) + _dot_nt(ops[s][3][:, sl], p_ref[s, h].astype(BF16))
                for (s, h, sl, sv), sc in zip(units, scores)]
        for s, h, sl, sv in units:
            p_ref[s, h] = p_ref[s, h] * ops[s][5][:, sl] + _dot_tn(ops[s][6][:, sv], ops[s][4][:, sl])
        for (s, h, sl, sv), o in zip(units, outs):
            r = slice(s * n, (s + 1) * n)
            o = o * lax.rsqrt(jnp.mean(o * o, -1, keepdims=True) + RMS_EPS) * ng
            og_s[r, sv] = (o * sg_s[r, sv]).astype(BF16)

    group = min(n_seq, HGRN_GROUP) if n == HGRN_CHUNK else n_seq
    for s0 in range(0, n_seq, group):
        recur(range(s0, s0 + group))
        rg = slice(s0 * n, (s0 + group) * n)
        x = x_ref[s0:s0 + group].reshape(group * n, d)
        y = _ln(ALPHA * x + _dot(og_s[rg, :], wout_ref[...]), g_ref[...], b_ref[...])
        o_ref[s0:s0 + group] = y.reshape(group, n, d)

    @pl.when(c == pl.num_programs(1) - 1)
    def _():
        for s in range(n_seq):
            for h in range(N_HEADS):
                s_out_ref[s, h] = p_ref[s, h].T


def _hgrn_mixer(x, w_in, lb_param, norm_g, w_out, g, b, s0, bsz, t, layer):
    m, d = x.shape
    dk = N_HEADS * HEAD_K
    chunk = min(t, HGRN_CHUNK)
    nc = t // chunk
    n_seq = min(bsz, HGRN_SEQS if nc > 1 else 2 * HGRN_SEQS)
    assert bsz % n_seq == 0 and t % chunk == 0
    rows = n_seq * chunk
    blk = lambda bi, ci: (bi, ci, 0)
    st = lambda bi, ci: (bi, 0, 0, 0)
    tri = jnp.tril(jnp.ones((chunk, chunk), BF16))
    in_specs = [pl.BlockSpec((n_seq, chunk, d), blk), _resident(w_in.shape), _resident(lb_param.shape),
                _resident(tri.shape), _resident((1, HEAD_V)), _resident(w_out.shape),
                _resident((1, d)), _resident((1, d))]
    args = [x.reshape(bsz, t, d), w_in, lb_param, tri, norm_g.reshape(1, HEAD_V), w_out,
            g.reshape(1, d), b.reshape(1, d)]
    if s0 is not None:
        in_specs.append(pl.BlockSpec((n_seq, N_HEADS, HEAD_K, HEAD_V), st))
        args.append(s0)
    y, s_fin = pl.pallas_call(
        functools.partial(_hgrn_mixer_kernel, chunk=chunk, n_seq=n_seq, layer=layer, has_init=s0 is not None),
        grid=(bsz // n_seq, nc),
        in_specs=in_specs,
        out_specs=[pl.BlockSpec((n_seq, chunk, d), blk),
                   pl.BlockSpec((n_seq, N_HEADS, HEAD_K, HEAD_V), st)],
        out_shape=[jax.ShapeDtypeStruct((bsz, t, d), F32),
                   jax.ShapeDtypeStruct((bsz, N_HEADS, HEAD_K, HEAD_V), F32)],
        scratch_shapes=[pltpu.VMEM((n_seq, N_HEADS, HEAD_V, HEAD_K), F32),
                        pltpu.VMEM((rows, dk), F32), pltpu.VMEM((rows, dk), F32), pltpu.VMEM((rows, dk), F32),
                        pltpu.VMEM((rows, d), BF16), pltpu.VMEM((rows, d), BF16), pltpu.VMEM((rows, d), BF16)],
        compiler_params=_params(2),
        name="hgrn_mixer",
    )(*args)
    return y.reshape(m, d), s_fin


def _conv_rows(ext_ref, wdw_ref, r0, rb):
    pad = CONV_HIST - (CONV_W - 1)
    cols = []
    for lb in range(ext_ref.shape[1] // LANES):
        ls = slice(lb * LANES, (lb + 1) * LANES)
        win = ext_ref[pl.ds(r0, rb + CONV_HIST), ls]
        y = None
        for r in range(SUBLANES):
            rows = rb + (SUBLANES if r else 0)
            z = None
            for a in range(CONV_HIST // SUBLANES + 1):
                j = SUBLANES * a + r - pad
                if 0 <= j < CONV_W:
                    term = wdw_ref[j:j + 1, ls] * win[SUBLANES * a:SUBLANES * a + rows, :]
                    z = term if z is None else z + term
            z = z[r:r + rb, :] if r else z
            y = z if y is None else y + z
        cols.append(y)
    return jnp.concatenate(cols, axis=1)


def _conv_mixer_kernel(*refs, tt, rb, n_seq, has_cache):
    if has_cache:
        (x_ref, cache_ref, w1_ref, b1_ref, wdw_ref, bdw_ref, lng_ref, lnb_ref, w2_ref, b2_ref, g_ref, b_ref,
         o_ref, nc_ref, ext_ref, act_ref) = refs
    else:
        (x_ref, w1_ref, b1_ref, wdw_ref, bdw_ref, lng_ref, lnb_ref, w2_ref, b2_ref, g_ref, b_ref,
         o_ref, nc_ref, ext_ref, act_ref) = refs
    ti = pl.program_id(1)
    d = x_ref.shape[-1]
    rows = n_seq * tt
    pad = CONV_HIST - (CONV_W - 1)

    @pl.when(ti == 0)
    def _():
        for s in range(n_seq):
            ext_ref[s, 0:CONV_HIST, :] = jnp.zeros((CONV_HIST, d), F32)
            if has_cache:
                ext_ref[s, pad:CONV_HIST, :] = cache_ref[s]

    @pl.when(ti > 0)
    def _():
        for s in range(n_seq):
            ext_ref[s, 0:CONV_HIST, :] = ext_ref[s, tt:tt + CONV_HIST, :]

    xb = x_ref[...].reshape(rows, d).astype(BF16)
    a = _dot(xb, w1_ref[:, 0:d]) + b1_ref[:, 0:d]
    gate = _dot(xb, w1_ref[:, d:]) + b1_ref[:, d:]
    u = a * jax.nn.sigmoid(gate)
    for s in range(n_seq):
        ext_ref[s, CONV_HIST:CONV_HIST + tt, :] = u[s * tt:(s + 1) * tt]

    def block(s, r0):
        y = _ln(_conv_rows(ext_ref.at[s], wdw_ref, r0, rb) + bdw_ref[...], lng_ref[...], lnb_ref[...])
        act_ref[pl.ds(s * tt + r0, rb), :] = _silu(y).astype(BF16)

    if n_seq == 1:
        def body(i, carry):
            block(0, pl.multiple_of(i * rb, rb))
            return carry
        lax.fori_loop(0, tt // rb, body, 0)
    else:
        for s in range(n_seq):
            for r0 in range(0, tt, rb):
                block(s, r0)
    mix = _dot(act_ref[...], w2_ref[...]) + b2_ref[...]
    y = _ln(ALPHA * x_ref[...].reshape(rows, d) + mix, g_ref[...], b_ref[...])
    o_ref[...] = y.reshape(n_seq, tt, d)

    @pl.when(ti == pl.num_programs(1) - 1)
    def _():
        for s in range(n_seq):
            nc_ref[s] = ext_ref[s, tt + pad:tt + CONV_HIST, :]


def _conv_mixer(x, cache, w1, b1, wdw, bdw, lng, lnb, w2, b2, g, b, bsz, t):
    m, d = x.shape
    tt = min(t, 512)
    nt = t // tt
    rb = min(tt, CONV_ROWS)
    n_seq = 1 if nt > 1 else min(bsz, CONV_SEQS)
    assert t % tt == 0 and tt % rb == 0 and bsz % n_seq == 0
    blk = lambda bi, ti: (bi, ti, 0)
    per_seq = lambda bi, ti: (bi, 0, 0)
    vec = lambda a: a.reshape(1, -1)
    in_specs = [pl.BlockSpec((n_seq, tt, d), blk)]
    args = [x.reshape(bsz, t, d)]
    if cache is not None:
        in_specs.append(pl.BlockSpec((n_seq, CONV_W - 1, d), per_seq))
        args.append(cache)
    consts = [w1, vec(b1), wdw, vec(bdw), vec(lng), vec(lnb), w2, vec(b2), vec(g), vec(b)]
    in_specs += [_resident(a.shape) for a in consts]
    y, new_cache = pl.pallas_call(
        functools.partial(_conv_mixer_kernel, tt=tt, rb=rb, n_seq=n_seq, has_cache=cache is not None),
        grid=(bsz // n_seq, nt),
        in_specs=in_specs,
        out_specs=[pl.BlockSpec((n_seq, tt, d), blk), pl.BlockSpec((n_seq, CONV_W - 1, d), per_seq)],
        out_shape=[jax.ShapeDtypeStruct((bsz, t, d), F32),
                   jax.ShapeDtypeStruct((bsz, CONV_W - 1, d), F32)],
        scratch_shapes=[pltpu.VMEM((n_seq, CONV_HIST + tt, d), F32), pltpu.VMEM((n_seq * tt, d), BF16)],
        compiler_params=_params(2),
        name="conv_mixer",
    )(*args, *consts)
    return y.reshape(m, d), new_cache


def _ffn(x, i, s, p, wb, casts=()):
    names = [f'g{i}{s}', f'u{i}{s}', f'd{i}{s}']
    g, b = p['ln_g'][i, 2 * s], p['ln_b'][i, 2 * s]
    if x.shape[0] <= FFN_SMALL_ROWS:
        if names[0] in wb:
            y, _ = _ffn_ln_small(x, *[wb[k] for k in names], (), g, b)
        else:
            y, cast = _ffn_ln_small(x, p['ffn_w_gate'], p['ffn_w_up'], p['ffn_w_down'], (i, s), g, b)
            wb.update(zip(names, cast))
        return y
    todo = [(name, src, lead) for name, src, lead in casts if name not in wb]
    y, cast = _ffn_ln(x, *[wb[k] for k in names], g, b, casts=[(src, lead) for _, src, lead in todo])
    wb.update({name: c for (name, _, _), c in zip(todo, cast)})
    return y


def _trunk(x3, hgrn_state, conv_cache, p, wb, x1=None):
    bsz, t, d = x3.shape
    ln = lambda i, s: (p['ln_g'][i, s], p['ln_b'][i, s])
    ffn = lambda x, i, s, casts: _ffn(x, i, s, p, wb, casts)

    def ffn_casts(i, s):
        return [(f'g{i}{s}', p['ffn_w_gate'], (i, s)), (f'u{i}{s}', p['ffn_w_up'], (i, s)),
                (f'd{i}{s}', p['ffn_w_down'], (i, s))]

    x = x1 if x1 is not None else ffn(
        x3.reshape(bsz * t, d), 0, 0,
        ffn_casts(0, 1) + [('w_in', p['hgrn_w_in'], (0,)), ('w_out', p['hgrn_w_out'], (0,))])
    x, h_new = _hgrn_mixer(x, wb['w_in'], p['hgrn_lb'], p['hgrn_norm_g'][0], wb['w_out'], *ln(0, 1),
                           None if hgrn_state is None else hgrn_state[0], bsz, t, 0)
    x = ffn(x, 0, 1, ffn_casts(1, 0) + [('pw1', p['conv_w_pw1'], (0,)), ('pw2', p['conv_w_pw2'], (0,))])
    x = ffn(x, 1, 0, ffn_casts(1, 1))
    x, c_new = _conv_mixer(x, None if conv_cache is None else conv_cache[0], wb['pw1'], p['conv_b_pw1'][0],
                           p['conv_w_dw'][0], p['conv_b_dw'][0], p['conv_ln_g'][0], p['conv_ln_b'][0],
                           wb['pw2'], p['conv_b_pw2'][0], *ln(1, 1), bsz, t)
    x = ffn(x, 1, 1, [])
    return x.reshape(bsz, t, d), h_new[None], c_new[None]


def kernel(x_prompt, x_sample, state_hgrn, cache_conv, ffn_w_gate, ffn_w_up, ffn_w_down, ln_g, ln_b, hgrn_w_in, hgrn_lb, hgrn_norm_g, hgrn_w_out, conv_w_pw1, conv_b_pw1, conv_w_dw, conv_b_dw, conv_ln_g, conv_ln_b, conv_w_pw2, conv_b_pw2):
    assert ffn_w_gate.shape[0] == DEPTH == 2 and hgrn_w_in.shape[0] == 1 and conv_w_pw1.shape[0] == 1
    p = {'ffn_w_gate': ffn_w_gate, 'ffn_w_up': ffn_w_up, 'ffn_w_down': ffn_w_down, 'ln_g': ln_g, 'ln_b': ln_b,
         'hgrn_w_in': hgrn_w_in, 'hgrn_lb': hgrn_lb, 'hgrn_norm_g': hgrn_norm_g, 'hgrn_w_out': hgrn_w_out,
         'conv_w_pw1': conv_w_pw1, 'conv_b_pw1': conv_b_pw1, 'conv_w_dw': conv_w_dw, 'conv_b_dw': conv_b_dw,
         'conv_ln_g': conv_ln_g, 'conv_ln_b': conv_ln_b, 'conv_w_pw2': conv_w_pw2, 'conv_b_pw2': conv_b_pw2}
    wb = {}
    d = x_sample.shape[-1]
    x1_s = _ffn(x_sample.reshape(-1, d), 0, 0, p, wb)
    y_p, h_p, c_p = _trunk(x_prompt, None, None, p, wb)
    y_s, h_s, c_s = _trunk(x_sample, state_hgrn, cache_conv, p, wb, x1=x1_s)
    return (y_p, y_s, h_p, h_s, c_p, c_s)
```

```python
import functools
import math

import jax
import jax.numpy as jnp
from jax import lax
from jax.experimental import pallas as pl
from jax.experimental.pallas import tpu as pltpu

F32 = jnp.float32
BF16 = jnp.bfloat16

DEPTH = 2
N_HEADS = 8
HEAD_K = 128
HEAD_V = 128
CONV_W = 31
HGRN_BLOCK = 16
ALPHA = (2 * DEPTH) ** 0.25
LN_EPS = 1e-5
RMS_EPS = 1e-5
LOG2E = math.log2(math.e)

V7X_VMEM_LIMIT_BYTES = 56 * 1024 * 1024
LANES = 128
SUBLANES = 8
BF16_ROWS = 16
CONV_HIST = 32
CONV_ROWS = 128
FFN_ROW_SPLIT = 2
HGRN_GROUP = 2
FFN_SMALL_COLS = 1024
FFN_SMALL_ROWS = 256
CONV_SEQS = 16
HGRN_CHUNK = 128
HGRN_SEQS = 4


def _row_tile(m):
    return 512 if m % 512 == 0 else m


def _resident(shape):
    return pl.BlockSpec(shape, lambda *_: (0,) * len(shape), pipeline_mode=pl.Buffered(1))


def _params(n_axes):
    return pltpu.CompilerParams(
        dimension_semantics=("arbitrary",) * n_axes,
        vmem_limit_bytes=V7X_VMEM_LIMIT_BYTES)


def _ln(y, g, b):
    mu = jnp.mean(y, -1, keepdims=True)
    d = y - mu
    var = jnp.mean(d * d, -1, keepdims=True)
    return d * lax.rsqrt(var + LN_EPS) * g + b


def _silu(x):
    return x * jax.nn.sigmoid(x)


def _dot(a, b):
    return jnp.dot(a, b, preferred_element_type=F32)


def _dot_nt(a, b):
    return lax.dot_general(a, b, (((1,), (1,)), ((), ())), preferred_element_type=F32)


def _dot_tn(a, b):
    return lax.dot_general(a, b, (((0,), (0,)), ((), ())), preferred_element_type=F32)


def _cast_specs(w, lead, steps):
    rows, cols = w.shape[-2:]
    nblk = min(steps, rows // BF16_ROWS)
    rb = rows // nblk
    assert rb * nblk == rows and rb % BF16_ROWS == 0
    blk = lambda i: jnp.minimum(i, nblk - 1)
    in_spec = pl.BlockSpec((None,) * len(lead) + (rb, cols), lambda i: tuple(lead) + (blk(i), 0))
    out_spec = pl.BlockSpec((rb, cols), lambda i: (blk(i), 0))
    return in_spec, out_spec, jax.ShapeDtypeStruct((rows, cols), BF16)


def _ffn_ln_kernel(*refs, fc, n_cast):
    x_ref, wg_ref, wu_ref, wd_ref, g_ref, b_ref = refs[:6]
    cast_in = refs[6:6 + n_cast]
    o_ref = refs[6 + n_cast]
    cast_out = refs[7 + n_cast:7 + 2 * n_cast]
    h_ref = refs[7 + 2 * n_cast]
    d_ff = wg_ref.shape[1]
    half = x_ref.shape[0] // FFN_ROW_SPLIT
    for hb in range(FFN_ROW_SPLIT):
        rs = slice(hb * half, (hb + 1) * half)
        x = x_ref[rs, :]
        xb = x.astype(BF16)
        for c in range(d_ff // fc):
            sl = slice(c * fc, (c + 1) * fc)
            h_ref[rs, sl] = (_silu(_dot(xb, wg_ref[:, sl])) * _dot(xb, wu_ref[:, sl])).astype(BF16)
        y = _dot(h_ref[rs, :], wd_ref[...])
        o_ref[rs, :] = _ln(ALPHA * x + 0.5 * y, g_ref[...], b_ref[...])
    for src, dst in zip(cast_in, cast_out):
        dst[...] = src[...].astype(BF16)


def _ffn_ln(x, wg, wu, wd, g, b, casts=()):
    m, d = x.shape
    d_ff = wg.shape[1]
    tm = _row_tile(m)
    steps = m // tm
    row = lambda i: (i, 0)
    cast_specs = [_cast_specs(w, lead, steps) for w, lead in casts]
    outs = pl.pallas_call(
        functools.partial(_ffn_ln_kernel, fc=512, n_cast=len(casts)),
        grid=(steps,),
        in_specs=[pl.BlockSpec((tm, d), row),
                  _resident((d, d_ff)), _resident((d, d_ff)), _resident((d_ff, d)),
                  _resident((1, d)), _resident((1, d))] + [s[0] for s in cast_specs],
        out_specs=[pl.BlockSpec((tm, d), row)] + [s[1] for s in cast_specs],
        out_shape=[jax.ShapeDtypeStruct((m, d), F32)] + [s[2] for s in cast_specs],
        scratch_shapes=[pltpu.VMEM((tm, d_ff), BF16)],
        compiler_params=_params(1),
        name="ffn_ln",
    )(x, wg, wu, wd, g.reshape(1, d), b.reshape(1, d), *[w for w, _ in casts])
    return outs[0], outs[1:]


def _ffn_small_kernel(*refs, emit):
    x_ref, wg_ref, wu_ref, wd_ref, g_ref, b_ref, o_ref = refs[:7]
    acc_ref = refs[-1]
    c = pl.program_id(0)
    wg, wu, wd = wg_ref[...].astype(BF16), wu_ref[...].astype(BF16), wd_ref[...].astype(BF16)
    if emit:
        for dst, w in zip(refs[7:10], (wg, wu, wd)):
            dst[...] = w
    x = x_ref[...]
    xb = x.astype(BF16)
    part = _dot((_silu(_dot(xb, wg)) * _dot(xb, wu)).astype(BF16), wd)

    @pl.when(c == 0)
    def _():
        acc_ref[...] = part

    @pl.when(c > 0)
    def _():
        acc_ref[...] += part

    @pl.when(c == pl.num_programs(0) - 1)
    def _():
        o_ref[...] = _ln(ALPHA * x + 0.5 * acc_ref[...], g_ref[...], b_ref[...])


def _ffn_ln_small(x, wg, wu, wd, lead, g, b):
    m, d = x.shape
    d_ff = wg.shape[-1]
    fc = FFN_SMALL_COLS
    emit = wg.dtype != BF16
    sq = (None,) * len(lead)
    col = lambda c: tuple(lead) + (0, c)
    row = lambda c: tuple(lead) + (c, 0)
    out_specs = [pl.BlockSpec((m, d), lambda c: (0, 0))]
    out_shape = [jax.ShapeDtypeStruct((m, d), F32)]
    if emit:
        out_specs += [pl.BlockSpec((d, fc), lambda c: (0, c)), pl.BlockSpec((d, fc), lambda c: (0, c)),
                      pl.BlockSpec((fc, d), lambda c: (c, 0))]
        out_shape += [jax.ShapeDtypeStruct((d, d_ff), BF16), jax.ShapeDtypeStruct((d, d_ff), BF16),
                      jax.ShapeDtypeStruct((d_ff, d), BF16)]
    outs = pl.pallas_call(
        functools.partial(_ffn_small_kernel, emit=emit),
        grid=(d_ff // fc,),
        in_specs=[_resident((m, d)), pl.BlockSpec(sq + (d, fc), col), pl.BlockSpec(sq + (d, fc), col),
                  pl.BlockSpec(sq + (fc, d), row), _resident((1, d)), _resident((1, d))],
        out_specs=out_specs,
        out_shape=out_shape,
        scratch_shapes=[pltpu.VMEM((m, d), F32)],
        compiler_params=_params(1),
        name="ffn_ln_small",
    )(x, wg, wu, wd, g.reshape(1, d), b.reshape(1, d))
    return outs[0], outs[1:]


def _hgrn_operands(qs, k, b2, n):
    levels = []
    l = n // 2
    while l >= HGRN_BLOCK:
        parts = []
        for j in range(n // l):
            r = slice(j * l, (j + 1) * l)
            a = (j | 1) * l - 1
            anchor = b2[a:a + 1, :]
            parts.append(qs[r] * jnp.exp2(b2[r] - anchor) if j & 1 else k[r] * jnp.exp2(anchor - b2[r]))
        levels.append((l, jnp.concatenate(parts, axis=0).astype(BF16)))
        l //= 2
    qd, kd = [], []
    for j in range(n // HGRN_BLOCK):
        r = slice(j * HGRN_BLOCK, (j + 1) * HGRN_BLOCK)
        m = j * HGRN_BLOCK + HGRN_BLOCK // 2
        rel = b2[r] - b2[m:m + 1, :]
        qd.append(qs[r] * jnp.exp2(rel))
        kd.append(k[r] * jnp.exp2(-rel))
    cat = lambda ps: (ps[0] if len(ps) == 1 else jnp.concatenate(ps, axis=0)).astype(BF16)
    b_last = b2[n - 1:n, :]
    q_in = (qs * jnp.exp2(b2)).astype(BF16)
    k_st = (k * jnp.exp2(b_last - b2)).astype(BF16)
    return levels, cat(qd), cat(kd), q_in, k_st, jnp.exp2(b_last)


def _hgrn_mixer_kernel(*refs, chunk, n_seq, layer, has_init):
    if has_init:
        (x_ref, win_ref, lb_ref, tri_ref, ng_ref, wout_ref, g_ref, b_ref, s0_ref, o_ref, s_out_ref,
         st_ref, qs_s, lf_s, k_s, v_s, sg_s, og_s) = refs
    else:
        (x_ref, win_ref, lb_ref, tri_ref, ng_ref, wout_ref, g_ref, b_ref, o_ref, s_out_ref,
         st_ref, qs_s, lf_s, k_s, v_s, sg_s, og_s) = refs
        s0_ref = None
    c = pl.program_id(1)
    n = chunk
    rows = n_seq * n
    d = x_ref.shape[-1]
    dk = N_HEADS * HEAD_K

    @pl.when(c == 0)
    def _():
        for s in range(n_seq):
            for h in range(N_HEADS):
                st_ref[s, h] = s0_ref[s, h] if has_init else jnp.zeros((HEAD_K, HEAD_V), F32)

    xb = x_ref[...].reshape(rows, d).astype(BF16)
    lbp = lb_ref[...]
    e = jnp.exp(lbp - jnp.max(lbp, axis=0, keepdims=True))
    lb = jnp.sum(e[:layer + 1], axis=0, keepdims=True) / jnp.sum(e, axis=0, keepdims=True)
    qs_s[...] = _silu(_dot(xb, win_ref[:, 0:dk])) * (HEAD_K ** -0.5)
    forget = lb + (1.0 - lb) * jax.nn.sigmoid(_dot(xb, win_ref[:, dk:2 * dk]))
    k_s[...] = 1.0 - forget
    lf_s[...] = jnp.log(forget)
    v_s[...] = _dot(xb, win_ref[:, 2 * dk:2 * dk + d]).astype(BF16)
    sg_s[...] = _silu(_dot(xb, win_ref[:, 2 * dk + d:])).astype(BF16)

    t_idx = lax.broadcasted_iota(jnp.int32, (n, n), 0)
    s_idx = lax.broadcasted_iota(jnp.int32, (n, n), 1)
    m_d = ((t_idx // HGRN_BLOCK) == (s_idx // HGRN_BLOCK)) & (s_idx <= t_idx)
    masks = {}
    l = n // 2
    while l >= HGRN_BLOCK:
        masks[l] = (((t_idx // l) & 1) == 1) & ((s_idx // l) == (t_idx // l) - 1)
        l //= 2
    ng = ng_ref[...]
    tri = tri_ref[...]
    heads = [(slice(h * HEAD_K, (h + 1) * HEAD_K), slice(h * HEAD_V, (h + 1) * HEAD_V)) for h in range(N_HEADS)]

    def recur(seqs):
        ops = {}
        for s in seqs:
            r = slice(s * n, (s + 1) * n)
            lf = lf_s[r, :]
            lf_hi = lf.astype(BF16)
            r1 = lf - lf_hi.astype(F32)
            lf_mid = r1.astype(BF16)
            lf_lo = (r1 - lf_mid.astype(F32)).astype(BF16)
            b2 = (_dot(tri, lf_hi) + _dot(tri, lf_mid) + _dot(tri, lf_lo)) * LOG2E
            ops[s] = _hgrn_operands(qs_s[r, :], k_s[r, :], b2, n) + (v_s[r, :],)
        units = [(s, h, sl, sv) for s in seqs for h, (sl, sv) in enumerate(heads)]
        scores = []
        for s, h, sl, sv in units:
            levels, q_d, k_d = ops[s][:3]
            sc = jnp.where(m_d, _dot_nt(q_d[:, sl], k_d[:, sl]), 0.0)
            for l, x_l in levels:
                sc = jnp.where(masks[l], _dot_nt(x_l[:, sl], x_l[:, sl]), sc)
            scores.append(sc.astype(BF16))
        outs = [_dot(jnp.concatenate([sc, ops[s][3][:, sl]], axis=1),
                     jnp.concatenate([ops[s][6][:, sv], st_ref[s, h].astype(BF16)], axis=0))
                for (s, h, sl, sv), sc in zip(units, scores)]
        for s, h, sl, sv in units:
            decay_col = jnp.broadcast_to(ops[s][5][:, sl], (HEAD_V, HEAD_K)).T
            st_ref[s, h] = st_ref[s, h] * decay_col + _dot_tn(ops[s][4][:, sl], ops[s][6][:, sv])
        for (s, h, sl, sv), o in zip(units, outs):
            r = slice(s * n, (s + 1) * n)
            o = o * lax.rsqrt(jnp.mean(o * o, -1, keepdims=True) + RMS_EPS) * ng
            og_s[r, sv] = (o * sg_s[r, sv]).astype(BF16)

    group, rec = (min(n_seq, HGRN_GROUP),) * 2 if n == HGRN_CHUNK else (n_seq, 1)
    for s0 in range(0, n_seq, group):
        for s in range(s0, s0 + group, rec):
            recur(range(s, s + rec))
        rg = slice(s0 * n, (s0 + group) * n)
        x = x_ref[s0:s0 + group].reshape(group * n, d)
        y = _ln(ALPHA * x + _dot(og_s[rg, :], wout_ref[...]), g_ref[...], b_ref[...])
        o_ref[s0:s0 + group] = y.reshape(group, n, d)

    @pl.when(c == pl.num_programs(1) - 1)
    def _():
        for s in range(n_seq):
            for h in range(N_HEADS):
                s_out_ref[s, h] = st_ref[s, h]


def _hgrn_mixer(x, w_in, lb_param, norm_g, w_out, g, b, s0, bsz, t, layer):
    m, d = x.shape
    dk = N_HEADS * HEAD_K
    chunk = min(t, HGRN_CHUNK)
    nc = t // chunk
    n_seq = min(bsz, HGRN_SEQS if nc > 1 else 2 * HGRN_SEQS)
    assert bsz % n_seq == 0 and t % chunk == 0
    rows = n_seq * chunk
    blk = lambda bi, ci: (bi, ci, 0)
    st = lambda bi, ci: (bi, 0, 0, 0)
    tri = jnp.tril(jnp.ones((chunk, chunk), BF16))
    in_specs = [pl.BlockSpec((n_seq, chunk, d), blk), _resident(w_in.shape), _resident(lb_param.shape),
                _resident(tri.shape), _resident((1, HEAD_V)), _resident(w_out.shape),
                _resident((1, d)), _resident((1, d))]
    args = [x.reshape(bsz, t, d), w_in, lb_param, tri, norm_g.reshape(1, HEAD_V), w_out,
            g.reshape(1, d), b.reshape(1, d)]
    if s0 is not None:
        in_specs.append(pl.BlockSpec((None, n_seq, N_HEADS, HEAD_K, HEAD_V), lambda bi, ci: (0, bi, 0, 0, 0)))
        args.append(s0)
    y, s_fin = pl.pallas_call(
        functools.partial(_hgrn_mixer_kernel, chunk=chunk, n_seq=n_seq, layer=layer, has_init=s0 is not None),
        grid=(bsz // n_seq, nc),
        in_specs=in_specs,
        out_specs=[pl.BlockSpec((n_seq, chunk, d), blk),
                   pl.BlockSpec((n_seq, N_HEADS, HEAD_K, HEAD_V), st)],
        out_shape=[jax.ShapeDtypeStruct((bsz, t, d), F32),
                   jax.ShapeDtypeStruct((bsz, N_HEADS, HEAD_K, HEAD_V), F32)],
        scratch_shapes=[pltpu.VMEM((n_seq, N_HEADS, HEAD_K, HEAD_V), F32),
                        pltpu.VMEM((rows, dk), F32), pltpu.VMEM((rows, dk), F32), pltpu.VMEM((rows, dk), F32),
                        pltpu.VMEM((rows, d), BF16), pltpu.VMEM((rows, d), BF16), pltpu.VMEM((rows, d), BF16)],
        compiler_params=_params(2),
        name="hgrn_mixer",
    )(*args)
    return y.reshape(m, d), s_fin


def _conv_rows(ext_ref, wdw_ref, r0, rb):
    pad = CONV_HIST - (CONV_W - 1)
    cols = []
    for lb in range(ext_ref.shape[1] // LANES):
        ls = slice(lb * LANES, (lb + 1) * LANES)
        win = ext_ref[pl.ds(r0, rb + CONV_HIST), ls]
        y = None
        for r in range(SUBLANES):
            rows = rb + (SUBLANES if r else 0)
            z = None
            for a in range(CONV_HIST // SUBLANES + 1):
                j = SUBLANES * a + r - pad
                if 0 <= j < CONV_W:
                    term = wdw_ref[j:j + 1, ls] * win[SUBLANES * a:SUBLANES * a + rows, :]
                    z = term if z is None else z + term
            z = z[r:r + rb, :] if r else z
            y = z if y is None else y + z
        cols.append(y)
    return jnp.concatenate(cols, axis=1)


def _conv_mixer_kernel(*refs, tt, rb, n_seq, has_cache):
    if has_cache:
        (x_ref, cache_ref, w1_ref, b1_ref, wdw_ref, bdw_ref, lng_ref, lnb_ref, w2_ref, b2_ref, g_ref, b_ref,
         o_ref, nc_ref, ext_ref, act_ref) = refs
    else:
        (x_ref, w1_ref, b1_ref, wdw_ref, bdw_ref, lng_ref, lnb_ref, w2_ref, b2_ref, g_ref, b_ref,
         o_ref, nc_ref, ext_ref, act_ref) = refs
    ti = pl.program_id(1)
    d = x_ref.shape[-1]
    rows = n_seq * tt
    pad = CONV_HIST - (CONV_W - 1)

    @pl.when(ti == 0)
    def _():
        for s in range(n_seq):
            ext_ref[s, 0:CONV_HIST, :] = jnp.zeros((CONV_HIST, d), F32)
            if has_cache:
                ext_ref[s, pad:CONV_HIST, :] = cache_ref[s]

    @pl.when(ti > 0)
    def _():
        for s in range(n_seq):
            ext_ref[s, 0:CONV_HIST, :] = ext_ref[s, tt:tt + CONV_HIST, :]

    xb = x_ref[...].reshape(rows, d).astype(BF16)
    a = _dot(xb, w1_ref[:, 0:d]) + b1_ref[:, 0:d]
    gate = _dot(xb, w1_ref[:, d:]) + b1_ref[:, d:]
    u = a * jax.nn.sigmoid(gate)
    for s in range(n_seq):
        ext_ref[s, CONV_HIST:CONV_HIST + tt, :] = u[s * tt:(s + 1) * tt]

    def block(s, r0):
        y = _ln(_conv_rows(ext_ref.at[s], wdw_ref, r0, rb) + bdw_ref[...], lng_ref[...], lnb_ref[...])
        act_ref[pl.ds(s * tt + r0, rb), :] = _silu(y).astype(BF16)

    if n_seq == 1:
        def body(i, carry):
            block(0, pl.multiple_of(i * rb, rb))
            return carry
        lax.fori_loop(0, tt // rb, body, 0)
    else:
        for s in range(n_seq):
            for r0 in range(0, tt, rb):
                block(s, r0)
    mix = _dot(act_ref[...], w2_ref[...]) + b2_ref[...]
    y = _ln(ALPHA * x_ref[...].reshape(rows, d) + mix, g_ref[...], b_ref[...])
    o_ref[...] = y.reshape(n_seq, tt, d)

    @pl.when(ti == pl.num_programs(1) - 1)
    def _():
        for s in range(n_seq):
            nc_ref[s] = ext_ref[s, tt + pad:tt + CONV_HIST, :]


def _conv_mixer(x, cache, w1, b1, wdw, bdw, lng, lnb, w2, b2, g, b, bsz, t):
    m, d = x.shape
    tt = min(t, 512)
    nt = t // tt
    rb = min(tt, CONV_ROWS)
    n_seq = 1 if nt > 1 else min(bsz, CONV_SEQS)
    assert t % tt == 0 and tt % rb == 0 and bsz % n_seq == 0
    blk = lambda bi, ti: (bi, ti, 0)
    per_seq = lambda bi, ti: (bi, 0, 0)
    vec = lambda a: a.reshape(1, -1)
    in_specs = [pl.BlockSpec((n_seq, tt, d), blk)]
    args = [x.reshape(bsz, t, d)]
    if cache is not None:
        in_specs.append(pl.BlockSpec((None, n_seq, CONV_W - 1, d), lambda bi, ti: (0, bi, 0, 0)))
        args.append(cache)
    consts = [w1, vec(b1), wdw, vec(bdw), vec(lng), vec(lnb), w2, vec(b2), vec(g), vec(b)]
    in_specs += [_resident(a.shape) for a in consts]
    y, new_cache = pl.pallas_call(
        functools.partial(_conv_mixer_kernel, tt=tt, rb=rb, n_seq=n_seq, has_cache=cache is not None),
        grid=(bsz // n_seq, nt),
        in_specs=in_specs,
        out_specs=[pl.BlockSpec((n_seq, tt, d), blk), pl.BlockSpec((n_seq, CONV_W - 1, d), per_seq)],
        out_shape=[jax.ShapeDtypeStruct((bsz, t, d), F32),
                   jax.ShapeDtypeStruct((bsz, CONV_W - 1, d), F32)],
        scratch_shapes=[pltpu.VMEM((n_seq, CONV_HIST + tt, d), F32), pltpu.VMEM((n_seq * tt, d), BF16)],
        compiler_params=_params(2),
        name="conv_mixer",
    )(*args, *consts)
    return y.reshape(m, d), new_cache


def _ffn(x, i, s, p, wb, casts=()):
    names = [f'g{i}{s}', f'u{i}{s}', f'd{i}{s}']
    g, b = p['ln_g'][i, 2 * s], p['ln_b'][i, 2 * s]
    if x.shape[0] <= FFN_SMALL_ROWS:
        if names[0] in wb:
            y, _ = _ffn_ln_small(x, *[wb[k] for k in names], (), g, b)
        else:
            y, cast = _ffn_ln_small(x, p['ffn_w_gate'], p['ffn_w_up'], p['ffn_w_down'], (i, s), g, b)
            wb.update(zip(names, cast))
        return y
    todo = [(name, src, lead) for name, src, lead in casts if name not in wb]
    y, cast = _ffn_ln(x, *[wb[k] for k in names], g, b, casts=[(src, lead) for _, src, lead in todo])
    wb.update({name: c for (name, _, _), c in zip(todo, cast)})
    return y


def _trunk(x3, hgrn_state, conv_cache, p, wb, x1=None):
    bsz, t, d = x3.shape
    ln = lambda i, s: (p['ln_g'][i, s], p['ln_b'][i, s])
    ffn = lambda x, i, s, casts: _ffn(x, i, s, p, wb, casts)

    def ffn_casts(i, s):
        return [(f'g{i}{s}', p['ffn_w_gate'], (i, s)), (f'u{i}{s}', p['ffn_w_up'], (i, s)),
                (f'd{i}{s}', p['ffn_w_down'], (i, s))]

    x = x1 if x1 is not None else ffn(
        x3.reshape(bsz * t, d), 0, 0,
        ffn_casts(0, 1) + [('w_in', p['hgrn_w_in'], (0,)), ('w_out', p['hgrn_w_out'], (0,))])
    x, h_new = _hgrn_mixer(x, wb['w_in'], p['hgrn_lb'], p['hgrn_norm_g'][0], wb['w_out'], *ln(0, 1),
                           hgrn_state, bsz, t, 0)
    x = ffn(x, 0, 1, ffn_casts(1, 0) + [('pw1', p['conv_w_pw1'], (0,)), ('pw2', p['conv_w_pw2'], (0,))])
    x = ffn(x, 1, 0, ffn_casts(1, 1))
    x, c_new = _conv_mixer(x, conv_cache, wb['pw1'], p['conv_b_pw1'][0],
                           p['conv_w_dw'][0], p['conv_b_dw'][0], p['conv_ln_g'][0], p['conv_ln_b'][0],
                           wb['pw2'], p['conv_b_pw2'][0], *ln(1, 1), bsz, t)
    x = ffn(x, 1, 1, [])
    return x.reshape(bsz, t, d), h_new[None], c_new[None]


def kernel(x_prompt, x_sample, state_hgrn, cache_conv, ffn_w_gate, ffn_w_up, ffn_w_down, ln_g, ln_b, hgrn_w_in, hgrn_lb, hgrn_norm_g, hgrn_w_out, conv_w_pw1, conv_b_pw1, conv_w_dw, conv_b_dw, conv_ln_g, conv_ln_b, conv_w_pw2, conv_b_pw2):
    assert ffn_w_gate.shape[0] == DEPTH == 2 and hgrn_w_in.shape[0] == 1 and conv_w_pw1.shape[0] == 1
    p = {'ffn_w_gate': ffn_w_gate, 'ffn_w_up': ffn_w_up, 'ffn_w_down': ffn_w_down, 'ln_g': ln_g, 'ln_b': ln_b,
         'hgrn_w_in': hgrn_w_in, 'hgrn_lb': hgrn_lb, 'hgrn_norm_g': hgrn_norm_g, 'hgrn_w_out': hgrn_w_out,
         'conv_w_pw1': conv_w_pw1, 'conv_b_pw1': conv_b_pw1, 'conv_w_dw': conv_w_dw, 'conv_b_dw': conv_b_dw,
         'conv_ln_g': conv_ln_g, 'conv_ln_b': conv_ln_b, 'conv_w_pw2': conv_w_pw2, 'conv_b_pw2': conv_b_pw2}
    wb = {}
    d = x_sample.shape[-1]
    x1_s = _ffn(x_sample.reshape(-1, d), 0, 0, p, wb)
    y_p, h_p, c_p = _trunk(x_prompt, None, None, p, wb)
    y_s, h_s, c_s = _trunk(x_sample, state_hgrn, cache_conv, p, wb, x1=x1_s)
    return (y_p, y_s, h_p, h_s, c_p, c_s)
```

```python
import functools
import math

import jax
import jax.numpy as jnp
from jax import lax
from jax.experimental import pallas as pl
from jax.experimental.pallas import tpu as pltpu

F32 = jnp.float32
BF16 = jnp.bfloat16

DEPTH = 2
N_HEADS = 8
HEAD_K = 128
HEAD_V = 128
CONV_W = 31
HGRN_BLOCK = 16
ALPHA = (2 * DEPTH) ** 0.25
LN_EPS = 1e-5
RMS_EPS = 1e-5
LOG2E = math.log2(math.e)

V7X_VMEM_LIMIT_BYTES = 56 * 1024 * 1024
LANES = 128
SUBLANES = 8
BF16_ROWS = 16
ROW_TILE = 512
CONV_TILE = 1024
CONV_HIST = 32
CONV_ROWS = 128
FFN_ROW_SPLIT = 2
HGRN_GROUP = 2
FFN_SMALL_COLS = 1024
FFN_SMALL_ROWS = 256
CONV_SEQS = 16
HGRN_CHUNK = 128
HGRN_SEQS = 4


def _row_tile(m):
    return ROW_TILE if m % ROW_TILE == 0 else m


def _resident(shape):
    return pl.BlockSpec(shape, lambda *_: (0,) * len(shape), pipeline_mode=pl.Buffered(1))


def _params(n_axes):
    return pltpu.CompilerParams(
        dimension_semantics=("arbitrary",) * n_axes,
        vmem_limit_bytes=V7X_VMEM_LIMIT_BYTES)


def _ln(y, g, b):
    mu = jnp.mean(y, -1, keepdims=True)
    d = y - mu
    var = jnp.mean(d * d, -1, keepdims=True)
    return d * lax.rsqrt(var + LN_EPS) * g + b


def _silu(x):
    return x * jax.nn.sigmoid(x)


def _dot(a, b):
    return jnp.dot(a, b, preferred_element_type=F32)


def _dot_nt(a, b):
    return lax.dot_general(a, b, (((1,), (1,)), ((), ())), preferred_element_type=F32)


def _dot_tn(a, b):
    return lax.dot_general(a, b, (((0,), (0,)), ((), ())), preferred_element_type=F32)


def _cast_specs(w, lead, steps):
    rows, cols = w.shape[-2:]
    nblk = min(steps, rows // BF16_ROWS)
    rb = rows // nblk
    assert rb * nblk == rows and rb % BF16_ROWS == 0
    blk = lambda i: jnp.minimum(i, nblk - 1)
    in_spec = pl.BlockSpec((None,) * len(lead) + (rb, cols), lambda i: tuple(lead) + (blk(i), 0))
    out_spec = pl.BlockSpec((rb, cols), lambda i: (blk(i), 0))
    return in_spec, out_spec, jax.ShapeDtypeStruct((rows, cols), BF16)


def _ffn_ln_kernel(*refs, fc, n_cast):
    x_ref, wg_ref, wu_ref, wd_ref, g_ref, b_ref = refs[:6]
    cast_in = refs[6:6 + n_cast]
    o_ref = refs[6 + n_cast]
    cast_out = refs[7 + n_cast:7 + 2 * n_cast]
    h_ref = refs[7 + 2 * n_cast]
    d_ff = wg_ref.shape[1]
    half = x_ref.shape[0] // FFN_ROW_SPLIT
    for hb in range(FFN_ROW_SPLIT):
        rs = slice(hb * half, (hb + 1) * half)
        x = x_ref[rs, :]
        xb = x.astype(BF16)
        for c in range(d_ff // fc):
            sl = slice(c * fc, (c + 1) * fc)
            h_ref[rs, sl] = (_silu(_dot(xb, wg_ref[:, sl])) * _dot(xb, wu_ref[:, sl])).astype(BF16)
        y = _dot(h_ref[rs, :], wd_ref[...])
        o_ref[rs, :] = _ln(ALPHA * x + 0.5 * y, g_ref[...], b_ref[...])
    for src, dst in zip(cast_in, cast_out):
        dst[...] = src[...].astype(BF16)


def _ffn_ln(x, wg, wu, wd, g, b, casts=()):
    m, d = x.shape
    d_ff = wg.shape[1]
    tm = _row_tile(m)
    steps = m // tm
    row = lambda i: (i, 0)
    cast_specs = [_cast_specs(w, lead, steps) for w, lead in casts]
    outs = pl.pallas_call(
        functools.partial(_ffn_ln_kernel, fc=512, n_cast=len(casts)),
        grid=(steps,),
        in_specs=[pl.BlockSpec((tm, d), row),
                  _resident((d, d_ff)), _resident((d, d_ff)), _resident((d_ff, d)),
                  _resident((1, d)), _resident((1, d))] + [s[0] for s in cast_specs],
        out_specs=[pl.BlockSpec((tm, d), row)] + [s[1] for s in cast_specs],
        out_shape=[jax.ShapeDtypeStruct((m, d), F32)] + [s[2] for s in cast_specs],
        scratch_shapes=[pltpu.VMEM((tm, d_ff), BF16)],
        compiler_params=_params(1),
        name="ffn_ln",
    )(x, wg, wu, wd, g.reshape(1, d), b.reshape(1, d), *[w for w, _ in casts])
    return outs[0], outs[1:]


def _ffn_small_kernel(*refs, emit):
    x_ref, wg_ref, wu_ref, wd_ref, g_ref, b_ref, o_ref = refs[:7]
    acc_ref = refs[-1]
    c = pl.program_id(0)
    wg, wu, wd = wg_ref[...].astype(BF16), wu_ref[...].astype(BF16), wd_ref[...].astype(BF16)
    if emit:
        for dst, w in zip(refs[7:10], (wg, wu, wd)):
            dst[...] = w
    x = x_ref[...]
    xb = x.astype(BF16)
    part = _dot((_silu(_dot(xb, wg)) * _dot(xb, wu)).astype(BF16), wd)

    @pl.when(c == 0)
    def _():
        acc_ref[...] = part

    @pl.when(c > 0)
    def _():
        acc_ref[...] += part

    @pl.when(c == pl.num_programs(0) - 1)
    def _():
        o_ref[...] = _ln(ALPHA * x + 0.5 * acc_ref[...], g_ref[...], b_ref[...])


def _ffn_ln_small(x, wg, wu, wd, lead, g, b):
    m, d = x.shape
    d_ff = wg.shape[-1]
    fc = FFN_SMALL_COLS
    emit = wg.dtype != BF16
    sq = (None,) * len(lead)
    col = lambda c: tuple(lead) + (0, c)
    row = lambda c: tuple(lead) + (c, 0)
    out_specs = [pl.BlockSpec((m, d), lambda c: (0, 0))]
    out_shape = [jax.ShapeDtypeStruct((m, d), F32)]
    if emit:
        out_specs += [pl.BlockSpec((d, fc), lambda c: (0, c)), pl.BlockSpec((d, fc), lambda c: (0, c)),
                      pl.BlockSpec((fc, d), lambda c: (c, 0))]
        out_shape += [jax.ShapeDtypeStruct((d, d_ff), BF16), jax.ShapeDtypeStruct((d, d_ff), BF16),
                      jax.ShapeDtypeStruct((d_ff, d), BF16)]
    outs = pl.pallas_call(
        functools.partial(_ffn_small_kernel, emit=emit),
        grid=(d_ff // fc,),
        in_specs=[_resident((m, d)), pl.BlockSpec(sq + (d, fc), col), pl.BlockSpec(sq + (d, fc), col),
                  pl.BlockSpec(sq + (fc, d), row), _resident((1, d)), _resident((1, d))],
        out_specs=out_specs,
        out_shape=out_shape,
        scratch_shapes=[pltpu.VMEM((m, d), F32)],
        compiler_params=_params(1),
        name="ffn_ln_small",
    )(x, wg, wu, wd, g.reshape(1, d), b.reshape(1, d))
    return outs[0], outs[1:]


def _hgrn_operands(qs, k, b2, n):
    levels = []
    l = n // 2
    while l >= HGRN_BLOCK:
        parts = []
        for j in range(n // l):
            r = slice(j * l, (j + 1) * l)
            a = (j | 1) * l - 1
            anchor = b2[a:a + 1, :]
            parts.append(qs[r] * jnp.exp2(b2[r] - anchor) if j & 1 else k[r] * jnp.exp2(anchor - b2[r]))
        levels.append((l, jnp.concatenate(parts, axis=0).astype(BF16)))
        l //= 2
    qd, kd = [], []
    for j in range(n // HGRN_BLOCK):
        r = slice(j * HGRN_BLOCK, (j + 1) * HGRN_BLOCK)
        m = j * HGRN_BLOCK + HGRN_BLOCK // 2
        rel = b2[r] - b2[m:m + 1, :]
        qd.append(qs[r] * jnp.exp2(rel))
        kd.append(k[r] * jnp.exp2(-rel))
    cat = lambda ps: (ps[0] if len(ps) == 1 else jnp.concatenate(ps, axis=0)).astype(BF16)
    b_last = b2[n - 1:n, :]
    q_in = (qs * jnp.exp2(b2)).astype(BF16)
    k_st = (k * jnp.exp2(b_last - b2)).astype(BF16)
    return levels, cat(qd), cat(kd), q_in, k_st, jnp.exp2(b_last)


def _hgrn_mixer_kernel(*refs, chunk, n_seq, layer, has_init):
    if has_init:
        (x_ref, win_ref, lb_ref, tri_ref, ng_ref, wout_ref, g_ref, b_ref, s0_ref, o_ref, s_out_ref,
         st_ref, qs_s, lf_s, k_s, v_s, sg_s, og_s) = refs
    else:
        (x_ref, win_ref, lb_ref, tri_ref, ng_ref, wout_ref, g_ref, b_ref, o_ref, s_out_ref,
         st_ref, qs_s, lf_s, k_s, v_s, sg_s, og_s) = refs
        s0_ref = None
    c = pl.program_id(1)
    n = chunk
    rows = n_seq * n
    d = x_ref.shape[-1]
    dk = N_HEADS * HEAD_K

    @pl.when(c == 0)
    def _():
        for s in range(n_seq):
            for h in range(N_HEADS):
                st_ref[s, h] = s0_ref[s, h] if has_init else jnp.zeros((HEAD_K, HEAD_V), F32)

    xb = x_ref[...].reshape(rows, d).astype(BF16)
    lbp = lb_ref[...]
    e = jnp.exp(lbp - jnp.max(lbp, axis=0, keepdims=True))
    lb = jnp.sum(e[:layer + 1], axis=0, keepdims=True) / jnp.sum(e, axis=0, keepdims=True)
    qs_s[...] = _silu(_dot(xb, win_ref[:, 0:dk])) * (HEAD_K ** -0.5)
    forget = lb + (1.0 - lb) * jax.nn.sigmoid(_dot(xb, win_ref[:, dk:2 * dk]))
    k_s[...] = 1.0 - forget
    lf_s[...] = jnp.log(forget)
    v_s[...] = _dot(xb, win_ref[:, 2 * dk:2 * dk + d]).astype(BF16)
    sg_s[...] = _silu(_dot(xb, win_ref[:, 2 * dk + d:])).astype(BF16)

    t_idx = lax.broadcasted_iota(jnp.int32, (n, n), 0)
    s_idx = lax.broadcasted_iota(jnp.int32, (n, n), 1)
    m_d = ((t_idx // HGRN_BLOCK) == (s_idx // HGRN_BLOCK)) & (s_idx <= t_idx)
    masks = {}
    l = n // 2
    while l >= HGRN_BLOCK:
        masks[l] = (((t_idx // l) & 1) == 1) & ((s_idx // l) == (t_idx // l) - 1)
        l //= 2
    ng = ng_ref[...]
    tri = tri_ref[...]
    heads = [(slice(h * HEAD_K, (h + 1) * HEAD_K), slice(h * HEAD_V, (h + 1) * HEAD_V)) for h in range(N_HEADS)]

    def recur(seqs):
        ops = {}
        for s in seqs:
            r = slice(s * n, (s + 1) * n)
            lf = lf_s[r, :]
            lf_hi = lf.astype(BF16)
            r1 = lf - lf_hi.astype(F32)
            lf_mid = r1.astype(BF16)
            lf_lo = (r1 - lf_mid.astype(F32)).astype(BF16)
            b2 = (_dot(tri, lf_hi) + _dot(tri, lf_mid) + _dot(tri, lf_lo)) * LOG2E
            ops[s] = _hgrn_operands(qs_s[r, :], k_s[r, :], b2, n) + (v_s[r, :],)
        units = [(s, h, sl, sv) for s in seqs for h, (sl, sv) in enumerate(heads)]
        scores = []
        for s, h, sl, sv in units:
            levels, q_d, k_d = ops[s][:3]
            sc = jnp.where(m_d, _dot_nt(q_d[:, sl], k_d[:, sl]), 0.0)
            for l, x_l in levels:
                sc = jnp.where(masks[l], _dot_nt(x_l[:, sl], x_l[:, sl]), sc)
            scores.append(sc.astype(BF16))
        outs = [_dot(jnp.concatenate([sc, ops[s][3][:, sl]], axis=1),
                     jnp.concatenate([ops[s][6][:, sv], st_ref[s, h].astype(BF16)], axis=0))
                for (s, h, sl, sv), sc in zip(units, scores)]
        for s, h, sl, sv in units:
            decay_col = jnp.broadcast_to(ops[s][5][:, sl], (HEAD_V, HEAD_K)).T
            st_ref[s, h] = st_ref[s, h] * decay_col + _dot_tn(ops[s][4][:, sl], ops[s][6][:, sv])
        for (s, h, sl, sv), o in zip(units, outs):
            r = slice(s * n, (s + 1) * n)
            o = o * lax.rsqrt(jnp.mean(o * o, -1, keepdims=True) + RMS_EPS) * ng
            og_s[r, sv] = (o * sg_s[r, sv]).astype(BF16)

    group, rec = (min(n_seq, HGRN_GROUP),) * 2 if n == HGRN_CHUNK else (n_seq, 1)
    for s0 in range(0, n_seq, group):
        for s in range(s0, s0 + group, rec):
            recur(range(s, s + rec))
        rg = slice(s0 * n, (s0 + group) * n)
        x = x_ref[s0:s0 + group].reshape(group * n, d)
        y = _ln(ALPHA * x + _dot(og_s[rg, :], wout_ref[...]), g_ref[...], b_ref[...])
        o_ref[s0:s0 + group] = y.reshape(group, n, d)

    @pl.when(c == pl.num_programs(1) - 1)
    def _():
        for s in range(n_seq):
            for h in range(N_HEADS):
                s_out_ref[s, h] = st_ref[s, h]


def _hgrn_mixer(x, w_in, lb_param, norm_g, w_out, g, b, s0, bsz, t, layer):
    m, d = x.shape
    dk = N_HEADS * HEAD_K
    chunk = min(t, HGRN_CHUNK)
    nc = t // chunk
    n_seq = min(bsz, HGRN_SEQS if nc > 1 else 2 * HGRN_SEQS)
    assert bsz % n_seq == 0 and t % chunk == 0
    rows = n_seq * chunk
    blk = lambda bi, ci: (bi, ci, 0)
    st = lambda bi, ci: (bi, 0, 0, 0)
    tri = jnp.tril(jnp.ones((chunk, chunk), BF16))
    in_specs = [pl.BlockSpec((n_seq, chunk, d), blk), _resident(w_in.shape), _resident(lb_param.shape),
                _resident(tri.shape), _resident((1, HEAD_V)), _resident(w_out.shape),
                _resident((1, d)), _resident((1, d))]
    args = [x.reshape(bsz, t, d), w_in, lb_param, tri, norm_g.reshape(1, HEAD_V), w_out,
            g.reshape(1, d), b.reshape(1, d)]
    if s0 is not None:
        in_specs.append(pl.BlockSpec((None, n_seq, N_HEADS, HEAD_K, HEAD_V), lambda bi, ci: (0, bi, 0, 0, 0)))
        args.append(s0)
    y, s_fin = pl.pallas_call(
        functools.partial(_hgrn_mixer_kernel, chunk=chunk, n_seq=n_seq, layer=layer, has_init=s0 is not None),
        grid=(bsz // n_seq, nc),
        in_specs=in_specs,
        out_specs=[pl.BlockSpec((n_seq, chunk, d), blk),
                   pl.BlockSpec((n_seq, N_HEADS, HEAD_K, HEAD_V), st)],
        out_shape=[jax.ShapeDtypeStruct((bsz, t, d), F32),
                   jax.ShapeDtypeStruct((bsz, N_HEADS, HEAD_K, HEAD_V), F32)],
        scratch_shapes=[pltpu.VMEM((n_seq, N_HEADS, HEAD_K, HEAD_V), F32),
                        pltpu.VMEM((rows, dk), F32), pltpu.VMEM((rows, dk), F32), pltpu.VMEM((rows, dk), F32),
                        pltpu.VMEM((rows, d), BF16), pltpu.VMEM((rows, d), BF16), pltpu.VMEM((rows, d), BF16)],
        compiler_params=_params(2),
        name="hgrn_mixer",
    )(*args)
    return y.reshape(m, d), s_fin


def _conv_rows(ext_ref, wdw_ref, r0, rb):
    pad = CONV_HIST - (CONV_W - 1)
    cols = []
    for lb in range(ext_ref.shape[1] // LANES):
        ls = slice(lb * LANES, (lb + 1) * LANES)
        win = ext_ref[pl.ds(r0, rb + CONV_HIST), ls]
        y = None
        for r in range(SUBLANES):
            rows = rb + (SUBLANES if r else 0)
            z = None
            for a in range(CONV_HIST // SUBLANES + 1):
                j = SUBLANES * a + r - pad
                if 0 <= j < CONV_W:
                    term = wdw_ref[j:j + 1, ls] * win[SUBLANES * a:SUBLANES * a + rows, :]
                    z = term if z is None else z + term
            z = z[r:r + rb, :] if r else z
            y = z if y is None else y + z
        cols.append(y)
    return jnp.concatenate(cols, axis=1)


def _conv_mixer_kernel(*refs, tt, rb, n_seq, has_cache):
    if has_cache:
        (x_ref, cache_ref, w1_ref, b1_ref, wdw_ref, bdw_ref, lng_ref, lnb_ref, w2_ref, b2_ref, g_ref, b_ref,
         o_ref, nc_ref, ext_ref, act_ref) = refs
    else:
        (x_ref, w1_ref, b1_ref, wdw_ref, bdw_ref, lng_ref, lnb_ref, w2_ref, b2_ref, g_ref, b_ref,
         o_ref, nc_ref, ext_ref, act_ref) = refs
    ti = pl.program_id(1)
    d = x_ref.shape[-1]
    rows = n_seq * tt
    pad = CONV_HIST - (CONV_W - 1)

    @pl.when(ti == 0)
    def _():
        for s in range(n_seq):
            ext_ref[s, 0:CONV_HIST, :] = jnp.zeros((CONV_HIST, d), F32)
            if has_cache:
                ext_ref[s, pad:CONV_HIST, :] = cache_ref[s]

    @pl.when(ti > 0)
    def _():
        for s in range(n_seq):
            ext_ref[s, 0:CONV_HIST, :] = ext_ref[s, tt:tt + CONV_HIST, :]

    xb = x_ref[...].reshape(rows, d).astype(BF16)
    a = _dot(xb, w1_ref[:, 0:d]) + b1_ref[:, 0:d]
    gate = _dot(xb, w1_ref[:, d:]) + b1_ref[:, d:]
    u = a * jax.nn.sigmoid(gate)
    for s in range(n_seq):
        ext_ref[s, CONV_HIST:CONV_HIST + tt, :] = u[s * tt:(s + 1) * tt]

    def block(s, r0):
        y = _ln(_conv_rows(ext_ref.at[s], wdw_ref, r0, rb) + bdw_ref[...], lng_ref[...], lnb_ref[...])
        act_ref[pl.ds(s * tt + r0, rb), :] = _silu(y).astype(BF16)

    if n_seq == 1:
        def body(i, carry):
            block(0, pl.multiple_of(i * rb, rb))
            return carry
        lax.fori_loop(0, tt // rb, body, 0)
    else:
        for s in range(n_seq):
            for r0 in range(0, tt, rb):
                block(s, r0)
    mix = _dot(act_ref[...], w2_ref[...]) + b2_ref[...]
    y = _ln(ALPHA * x_ref[...].reshape(rows, d) + mix, g_ref[...], b_ref[...])
    o_ref[...] = y.reshape(n_seq, tt, d)

    @pl.when(ti == pl.num_programs(1) - 1)
    def _():
        for s in range(n_seq):
            nc_ref[s] = ext_ref[s, tt + pad:tt + CONV_HIST, :]


def _conv_mixer(x, cache, w1, b1, wdw, bdw, lng, lnb, w2, b2, g, b, bsz, t):
    m, d = x.shape
    tt = min(t, CONV_TILE)
    nt = t // tt
    rb = min(tt, CONV_ROWS)
    n_seq = 1 if nt > 1 else min(bsz, CONV_SEQS)
    assert t % tt == 0 and tt % rb == 0 and bsz % n_seq == 0
    blk = lambda bi, ti: (bi, ti, 0)
    per_seq = lambda bi, ti: (bi, 0, 0)
    vec = lambda a: a.reshape(1, -1)
    in_specs = [pl.BlockSpec((n_seq, tt, d), blk)]
    args = [x.reshape(bsz, t, d)]
    if cache is not None:
        in_specs.append(pl.BlockSpec((None, n_seq, CONV_W - 1, d), lambda bi, ti: (0, bi, 0, 0)))
        args.append(cache)
    consts = [w1, vec(b1), wdw, vec(bdw), vec(lng), vec(lnb), w2, vec(b2), vec(g), vec(b)]
    in_specs += [_resident(a.shape) for a in consts]
    y, new_cache = pl.pallas_call(
        functools.partial(_conv_mixer_kernel, tt=tt, rb=rb, n_seq=n_seq, has_cache=cache is not None),
        grid=(bsz // n_seq, nt),
        in_specs=in_specs,
        out_specs=[pl.BlockSpec((n_seq, tt, d), blk), pl.BlockSpec((n_seq, CONV_W - 1, d), per_seq)],
        out_shape=[jax.ShapeDtypeStruct((bsz, t, d), F32),
                   jax.ShapeDtypeStruct((bsz, CONV_W - 1, d), F32)],
        scratch_shapes=[pltpu.VMEM((n_seq, CONV_HIST + tt, d), F32), pltpu.VMEM((n_seq * tt, d), BF16)],
        compiler_params=_params(2),
        name="conv_mixer",
    )(*args, *consts)
    return y.reshape(m, d), new_cache


def _ffn(x, i, s, p, wb, casts=()):
    names = [f'g{i}{s}', f'u{i}{s}', f'd{i}{s}']
    g, b = p['ln_g'][i, 2 * s], p['ln_b'][i, 2 * s]
    if x.shape[0] <= FFN_SMALL_ROWS:
        if names[0] in wb:
            y, _ = _ffn_ln_small(x, *[wb[k] for k in names], (), g, b)
        else:
            y, cast = _ffn_ln_small(x, p['ffn_w_gate'], p['ffn_w_up'], p['ffn_w_down'], (i, s), g, b)
            wb.update(zip(names, cast))
        return y
    todo = [(name, src, lead) for name, src, lead in casts if name not in wb]
    y, cast = _ffn_ln(x, *[wb[k] for k in names], g, b, casts=[(src, lead) for _, src, lead in todo])
    wb.update({name: c for (name, _, _), c in zip(todo, cast)})
    return y


def _trunk(x3, hgrn_state, conv_cache, p, wb, x1=None):
    bsz, t, d = x3.shape
    ln = lambda i, s: (p['ln_g'][i, s], p['ln_b'][i, s])
    ffn = lambda x, i, s, casts: _ffn(x, i, s, p, wb, casts)

    def ffn_casts(i, s):
        return [(f'g{i}{s}', p['ffn_w_gate'], (i, s)), (f'u{i}{s}', p['ffn_w_up'], (i, s)),
                (f'd{i}{s}', p['ffn_w_down'], (i, s))]

    x = x1 if x1 is not None else ffn(
        x3.reshape(bsz * t, d), 0, 0,
        ffn_casts(0, 1) + [('w_in', p['hgrn_w_in'], (0,)), ('w_out', p['hgrn_w_out'], (0,))])
    x, h_new = _hgrn_mixer(x, wb['w_in'], p['hgrn_lb'], p['hgrn_norm_g'][0], wb['w_out'], *ln(0, 1),
                           hgrn_state, bsz, t, 0)
    x = ffn(x, 0, 1, ffn_casts(1, 0) + [('pw1', p['conv_w_pw1'], (0,)), ('pw2', p['conv_w_pw2'], (0,))])
    x = ffn(x, 1, 0, ffn_casts(1, 1))
    x, c_new = _conv_mixer(x, conv_cache, wb['pw1'], p['conv_b_pw1'][0],
                           p['conv_w_dw'][0], p['conv_b_dw'][0], p['conv_ln_g'][0], p['conv_ln_b'][0],
                           wb['pw2'], p['conv_b_pw2'][0], *ln(1, 1), bsz, t)
    x = ffn(x, 1, 1, [])
    return x.reshape(bsz, t, d), h_new[None], c_new[None]


def kernel(x_prompt, x_sample, state_hgrn, cache_conv, ffn_w_gate, ffn_w_up, ffn_w_down, ln_g, ln_b, hgrn_w_in, hgrn_lb, hgrn_norm_g, hgrn_w_out, conv_w_pw1, conv_b_pw1, conv_w_dw, conv_b_dw, conv_ln_g, conv_ln_b, conv_w_pw2, conv_b_pw2):
    assert ffn_w_gate.shape[0] == DEPTH == 2 and hgrn_w_in.shape[0] == 1 and conv_w_pw1.shape[0] == 1
    p = {'ffn_w_gate': ffn_w_gate, 'ffn_w_up': ffn_w_up, 'ffn_w_down': ffn_w_down, 'ln_g': ln_g, 'ln_b': ln_b,
         'hgrn_w_in': hgrn_w_in, 'hgrn_lb': hgrn_lb, 'hgrn_norm_g': hgrn_norm_g, 'hgrn_w_out': hgrn_w_out,
         'conv_w_pw1': conv_w_pw1, 'conv_b_pw1': conv_b_pw1, 'conv_w_dw': conv_w_dw, 'conv_b_dw': conv_b_dw,
         'conv_ln_g': conv_ln_g, 'conv_ln_b': conv_ln_b, 'conv_w_pw2': conv_w_pw2, 'conv_b_pw2': conv_b_pw2}
    wb = {}
    d = x_sample.shape[-1]
    x1_s = _ffn(x_sample.reshape(-1, d), 0, 0, p, wb)
    y_p, h_p, c_p = _trunk(x_prompt, None, None, p, wb)
    y_s, h_s, c_s = _trunk(x_sample, state_hgrn, cache_conv, p, wb, x1=x1_s)
    return (y_p, y_s, h_p, h_s, c_p, c_s)
```

```python
import functools
import math

import jax
import jax.numpy as jnp
from jax import lax
from jax.experimental import pallas as pl
from jax.experimental.pallas import tpu as pltpu

F32 = jnp.float32
BF16 = jnp.bfloat16

DEPTH = 2
N_HEADS = 8
HEAD_K = 128
HEAD_V = 128
CONV_W = 31
HGRN_BLOCK = 16
ALPHA = (2 * DEPTH) ** 0.25
LN_EPS = 1e-5
RMS_EPS = 1e-5
LOG2E = math.log2(math.e)

V7X_VMEM_LIMIT_BYTES = 56 * 1024 * 1024
LANES = 128
SUBLANES = 8
BF16_ROWS = 16
ROW_TILE = 512
CONV_TILE = 1024
CONV_HIST = 32
CONV_ROWS = 256
FFN_COLS = 512
FFN_ROW_SPLIT = 2
HGRN_GROUP = 2
FFN_SMALL_COLS = 1024
FFN_SMALL_ROWS = 256
CONV_SEQS = 16
HGRN_CHUNK = 128
HGRN_SEQS = 4


def _row_tile(m):
    return ROW_TILE if m % ROW_TILE == 0 else m


def _resident(shape):
    return pl.BlockSpec(shape, lambda *_: (0,) * len(shape), pipeline_mode=pl.Buffered(1))


def _params(n_axes):
    return pltpu.CompilerParams(
        dimension_semantics=("arbitrary",) * n_axes,
        vmem_limit_bytes=V7X_VMEM_LIMIT_BYTES)


def _ln(y, g, b):
    mu = jnp.mean(y, -1, keepdims=True)
    d = y - mu
    var = jnp.mean(d * d, -1, keepdims=True)
    return d * lax.rsqrt(var + LN_EPS) * g + b


def _silu(x):
    return x * jax.nn.sigmoid(x)


def _dot(a, b):
    return jnp.dot(a, b, preferred_element_type=F32)


def _dot_nt(a, b):
    return lax.dot_general(a, b, (((1,), (1,)), ((), ())), preferred_element_type=F32)


def _dot_tn(a, b):
    return lax.dot_general(a, b, (((0,), (0,)), ((), ())), preferred_element_type=F32)


def _cast_specs(w, lead, steps):
    rows, cols = w.shape[-2:]
    nblk = min(steps, rows // BF16_ROWS)
    rb = rows // nblk
    assert rb * nblk == rows and rb % BF16_ROWS == 0
    blk = lambda i: jnp.minimum(i, nblk - 1)
    in_spec = pl.BlockSpec((None,) * len(lead) + (rb, cols), lambda i: tuple(lead) + (blk(i), 0))
    out_spec = pl.BlockSpec((rb, cols), lambda i: (blk(i), 0))
    return in_spec, out_spec, jax.ShapeDtypeStruct((rows, cols), BF16)


def _ffn_ln_kernel(*refs, fc, n_cast):
    x_ref, wg_ref, wu_ref, wd_ref, g_ref, b_ref = refs[:6]
    cast_in = refs[6:6 + n_cast]
    o_ref = refs[6 + n_cast]
    cast_out = refs[7 + n_cast:7 + 2 * n_cast]
    h_ref = refs[7 + 2 * n_cast]
    d_ff = wg_ref.shape[1]
    half = x_ref.shape[0] // FFN_ROW_SPLIT
    for hb in range(FFN_ROW_SPLIT):
        rs = slice(hb * half, (hb + 1) * half)
        x = x_ref[rs, :]
        xb = x.astype(BF16)
        for c in range(d_ff // fc):
            sl = slice(c * fc, (c + 1) * fc)
            h_ref[rs, sl] = (_silu(_dot(xb, wg_ref[:, sl])) * _dot(xb, wu_ref[:, sl])).astype(BF16)
        y = _dot(h_ref[rs, :], wd_ref[...])
        o_ref[rs, :] = _ln(ALPHA * x + 0.5 * y, g_ref[...], b_ref[...])
    for src, dst in zip(cast_in, cast_out):
        dst[...] = src[...].astype(BF16)


def _ffn_ln(x, wg, wu, wd, g, b, casts=()):
    m, d = x.shape
    d_ff = wg.shape[1]
    tm = _row_tile(m)
    steps = m // tm
    row = lambda i: (i, 0)
    cast_specs = [_cast_specs(w, lead, steps) for w, lead in casts]
    outs = pl.pallas_call(
        functools.partial(_ffn_ln_kernel, fc=FFN_COLS, n_cast=len(casts)),
        grid=(steps,),
        in_specs=[pl.BlockSpec((tm, d), row),
                  _resident((d, d_ff)), _resident((d, d_ff)), _resident((d_ff, d)),
                  _resident((1, d)), _resident((1, d))] + [s[0] for s in cast_specs],
        out_specs=[pl.BlockSpec((tm, d), row)] + [s[1] for s in cast_specs],
        out_shape=[jax.ShapeDtypeStruct((m, d), F32)] + [s[2] for s in cast_specs],
        scratch_shapes=[pltpu.VMEM((tm, d_ff), BF16)],
        compiler_params=_params(1),
        name="ffn_ln",
    )(x, wg, wu, wd, g.reshape(1, d), b.reshape(1, d), *[w for w, _ in casts])
    return outs[0], outs[1:]


def _ffn_small_kernel(*refs, emit):
    x_ref, wg_ref, wu_ref, wd_ref, g_ref, b_ref, o_ref = refs[:7]
    acc_ref = refs[-1]
    c = pl.program_id(0)
    wg, wu, wd = wg_ref[...].astype(BF16), wu_ref[...].astype(BF16), wd_ref[...].astype(BF16)
    if emit:
        for dst, w in zip(refs[7:10], (wg, wu, wd)):
            dst[...] = w
    x = x_ref[...]
    xb = x.astype(BF16)
    part = _dot((_silu(_dot(xb, wg)) * _dot(xb, wu)).astype(BF16), wd)

    @pl.when(c == 0)
    def _():
        acc_ref[...] = part

    @pl.when(c > 0)
    def _():
        acc_ref[...] += part

    @pl.when(c == pl.num_programs(0) - 1)
    def _():
        o_ref[...] = _ln(ALPHA * x + 0.5 * acc_ref[...], g_ref[...], b_ref[...])


def _ffn_ln_small(x, wg, wu, wd, lead, g, b):
    m, d = x.shape
    d_ff = wg.shape[-1]
    fc = FFN_SMALL_COLS
    emit = wg.dtype != BF16
    sq = (None,) * len(lead)
    col = lambda c: tuple(lead) + (0, c)
    row = lambda c: tuple(lead) + (c, 0)
    out_specs = [pl.BlockSpec((m, d), lambda c: (0, 0))]
    out_shape = [jax.ShapeDtypeStruct((m, d), F32)]
    if emit:
        out_specs += [pl.BlockSpec((d, fc), lambda c: (0, c)), pl.BlockSpec((d, fc), lambda c: (0, c)),
                      pl.BlockSpec((fc, d), lambda c: (c, 0))]
        out_shape += [jax.ShapeDtypeStruct((d, d_ff), BF16), jax.ShapeDtypeStruct((d, d_ff), BF16),
                      jax.ShapeDtypeStruct((d_ff, d), BF16)]
    outs = pl.pallas_call(
        functools.partial(_ffn_small_kernel, emit=emit),
        grid=(d_ff // fc,),
        in_specs=[_resident((m, d)), pl.BlockSpec(sq + (d, fc), col), pl.BlockSpec(sq + (d, fc), col),
                  pl.BlockSpec(sq + (fc, d), row), _resident((1, d)), _resident((1, d))],
        out_specs=out_specs,
        out_shape=out_shape,
        scratch_shapes=[pltpu.VMEM((m, d), F32)],
        compiler_params=_params(1),
        name="ffn_ln_small",
    )(x, wg, wu, wd, g.reshape(1, d), b.reshape(1, d))
    return outs[0], outs[1:]


def _hgrn_operands(qs, k, b2, n):
    levels = []
    l = n // 2
    while l >= HGRN_BLOCK:
        parts = []
        for j in range(n // l):
            r = slice(j * l, (j + 1) * l)
            a = (j | 1) * l - 1
            anchor = b2[a:a + 1, :]
            parts.append(qs[r] * jnp.exp2(b2[r] - anchor) if j & 1 else k[r] * jnp.exp2(anchor - b2[r]))
        levels.append((l, jnp.concatenate(parts, axis=0).astype(BF16)))
        l //= 2
    qd, kd = [], []
    for j in range(n // HGRN_BLOCK):
        r = slice(j * HGRN_BLOCK, (j + 1) * HGRN_BLOCK)
        m = j * HGRN_BLOCK + HGRN_BLOCK // 2
        rel = b2[r] - b2[m:m + 1, :]
        qd.append(qs[r] * jnp.exp2(rel))
        kd.append(k[r] * jnp.exp2(-rel))
    cat = lambda ps: (ps[0] if len(ps) == 1 else jnp.concatenate(ps, axis=0)).astype(BF16)
    b_last = b2[n - 1:n, :]
    q_in = (qs * jnp.exp2(b2)).astype(BF16)
    k_st = (k * jnp.exp2(b_last - b2)).astype(BF16)
    return levels, cat(qd), cat(kd), q_in, k_st, jnp.exp2(b_last)


def _hgrn_mixer_kernel(*refs, chunk, n_seq, layer, has_init):
    if has_init:
        (x_ref, win_ref, lb_ref, tri_ref, ng_ref, wout_ref, g_ref, b_ref, s0_ref, o_ref, s_out_ref,
         st_ref, qs_s, lf_s, k_s, v_s, sg_s, og_s) = refs
    else:
        (x_ref, win_ref, lb_ref, tri_ref, ng_ref, wout_ref, g_ref, b_ref, o_ref, s_out_ref,
         st_ref, qs_s, lf_s, k_s, v_s, sg_s, og_s) = refs
        s0_ref = None
    c = pl.program_id(1)
    n = chunk
    rows = n_seq * n
    d = x_ref.shape[-1]
    dk = N_HEADS * HEAD_K

    @pl.when(c == 0)
    def _():
        for s in range(n_seq):
            for h in range(N_HEADS):
                st_ref[s, h] = s0_ref[s, h] if has_init else jnp.zeros((HEAD_K, HEAD_V), F32)

    xb = x_ref[...].reshape(rows, d).astype(BF16)
    lbp = lb_ref[...]
    e = jnp.exp(lbp - jnp.max(lbp, axis=0, keepdims=True))
    lb = jnp.sum(e[:layer + 1], axis=0, keepdims=True) / jnp.sum(e, axis=0, keepdims=True)
    qs_s[...] = _silu(_dot(xb, win_ref[:, 0:dk])) * (HEAD_K ** -0.5)
    forget = lb + (1.0 - lb) * jax.nn.sigmoid(_dot(xb, win_ref[:, dk:2 * dk]))
    k_s[...] = 1.0 - forget
    lf_s[...] = jnp.log(forget)
    v_s[...] = _dot(xb, win_ref[:, 2 * dk:2 * dk + d]).astype(BF16)
    sg_s[...] = _silu(_dot(xb, win_ref[:, 2 * dk + d:])).astype(BF16)

    t_idx = lax.broadcasted_iota(jnp.int32, (n, n), 0)
    s_idx = lax.broadcasted_iota(jnp.int32, (n, n), 1)
    m_d = ((t_idx // HGRN_BLOCK) == (s_idx // HGRN_BLOCK)) & (s_idx <= t_idx)
    masks = {}
    l = n // 2
    while l >= HGRN_BLOCK:
        masks[l] = (((t_idx // l) & 1) == 1) & ((s_idx // l) == (t_idx // l) - 1)
        l //= 2
    ng = ng_ref[...]
    tri = tri_ref[...]
    heads = [(slice(h * HEAD_K, (h + 1) * HEAD_K), slice(h * HEAD_V, (h + 1) * HEAD_V)) for h in range(N_HEADS)]

    def recur(seqs):
        ops = {}
        for s in seqs:
            r = slice(s * n, (s + 1) * n)
            lf = lf_s[r, :]
            lf_hi = lf.astype(BF16)
            r1 = lf - lf_hi.astype(F32)
            lf_mid = r1.astype(BF16)
            lf_lo = (r1 - lf_mid.astype(F32)).astype(BF16)
            b2 = (_dot(tri, lf_hi) + _dot(tri, lf_mid) + _dot(tri, lf_lo)) * LOG2E
            ops[s] = _hgrn_operands(qs_s[r, :], k_s[r, :], b2, n) + (v_s[r, :],)
        units = [(s, h, sl, sv) for s in seqs for h, (sl, sv) in enumerate(heads)]
        scores = []
        for s, h, sl, sv in units:
            levels, q_d, k_d = ops[s][:3]
            sc = jnp.where(m_d, _dot_nt(q_d[:, sl], k_d[:, sl]), 0.0)
            for l, x_l in levels:
                sc = jnp.where(masks[l], _dot_nt(x_l[:, sl], x_l[:, sl]), sc)
            scores.append(sc.astype(BF16))
        outs = [_dot(jnp.concatenate([sc, ops[s][3][:, sl]], axis=1),
                     jnp.concatenate([ops[s][6][:, sv], st_ref[s, h].astype(BF16)], axis=0))
                for (s, h, sl, sv), sc in zip(units, scores)]
        for s, h, sl, sv in units:
            decay_col = jnp.broadcast_to(ops[s][5][:, sl], (HEAD_V, HEAD_K)).T
            st_ref[s, h] = st_ref[s, h] * decay_col + _dot_tn(ops[s][4][:, sl], ops[s][6][:, sv])
        for (s, h, sl, sv), o in zip(units, outs):
            r = slice(s * n, (s + 1) * n)
            o = o * lax.rsqrt(jnp.mean(o * o, -1, keepdims=True) + RMS_EPS) * ng
            og_s[r, sv] = (o * sg_s[r, sv]).astype(BF16)

    group, rec = (min(n_seq, HGRN_GROUP),) * 2 if n == HGRN_CHUNK else (n_seq, 1)
    for s0 in range(0, n_seq, group):
        for s in range(s0, s0 + group, rec):
            recur(range(s, s + rec))
        rg = slice(s0 * n, (s0 + group) * n)
        x = x_ref[s0:s0 + group].reshape(group * n, d)
        y = _ln(ALPHA * x + _dot(og_s[rg, :], wout_ref[...]), g_ref[...], b_ref[...])
        o_ref[s0:s0 + group] = y.reshape(group, n, d)

    @pl.when(c == pl.num_programs(1) - 1)
    def _():
        for s in range(n_seq):
            for h in range(N_HEADS):
                s_out_ref[s, h] = st_ref[s, h]


def _hgrn_mixer(x, w_in, lb_param, norm_g, w_out, g, b, s0, bsz, t, layer):
    m, d = x.shape
    dk = N_HEADS * HEAD_K
    chunk = min(t, HGRN_CHUNK)
    nc = t // chunk
    n_seq = min(bsz, HGRN_SEQS if nc > 1 else 2 * HGRN_SEQS)
    assert bsz % n_seq == 0 and t % chunk == 0
    rows = n_seq * chunk
    blk = lambda bi, ci: (bi, ci, 0)
    st = lambda bi, ci: (bi, 0, 0, 0)
    tri = jnp.tril(jnp.ones((chunk, chunk), BF16))
    in_specs = [pl.BlockSpec((n_seq, chunk, d), blk), _resident(w_in.shape), _resident(lb_param.shape),
                _resident(tri.shape), _resident((1, HEAD_V)), _resident(w_out.shape),
                _resident((1, d)), _resident((1, d))]
    args = [x.reshape(bsz, t, d), w_in, lb_param, tri, norm_g.reshape(1, HEAD_V), w_out,
            g.reshape(1, d), b.reshape(1, d)]
    if s0 is not None:
        in_specs.append(pl.BlockSpec((None, n_seq, N_HEADS, HEAD_K, HEAD_V), lambda bi, ci: (0, bi, 0, 0, 0)))
        args.append(s0)
    y, s_fin = pl.pallas_call(
        functools.partial(_hgrn_mixer_kernel, chunk=chunk, n_seq=n_seq, layer=layer, has_init=s0 is not None),
        grid=(bsz // n_seq, nc),
        in_specs=in_specs,
        out_specs=[pl.BlockSpec((n_seq, chunk, d), blk),
                   pl.BlockSpec((n_seq, N_HEADS, HEAD_K, HEAD_V), st)],
        out_shape=[jax.ShapeDtypeStruct((bsz, t, d), F32),
                   jax.ShapeDtypeStruct((bsz, N_HEADS, HEAD_K, HEAD_V), F32)],
        scratch_shapes=[pltpu.VMEM((n_seq, N_HEADS, HEAD_K, HEAD_V), F32),
                        pltpu.VMEM((rows, dk), F32), pltpu.VMEM((rows, dk), F32), pltpu.VMEM((rows, dk), F32),
                        pltpu.VMEM((rows, d), BF16), pltpu.VMEM((rows, d), BF16), pltpu.VMEM((rows, d), BF16)],
        compiler_params=_params(2),
        name="hgrn_mixer",
    )(*args)
    return y.reshape(m, d), s_fin


def _conv_rows(ext_ref, wdw_ref, r0, rb):
    pad = CONV_HIST - (CONV_W - 1)
    cols = []
    for lb in range(ext_ref.shape[1] // LANES):
        ls = slice(lb * LANES, (lb + 1) * LANES)
        win = ext_ref[pl.ds(r0, rb + CONV_HIST), ls]
        y = None
        for r in range(SUBLANES):
            rows = rb + (SUBLANES if r else 0)
            z = None
            for a in range(CONV_HIST // SUBLANES + 1):
                j = SUBLANES * a + r - pad
                if 0 <= j < CONV_W:
                    term = wdw_ref[j:j + 1, ls] * win[SUBLANES * a:SUBLANES * a + rows, :]
                    z = term if z is None else z + term
            z = z[r:r + rb, :] if r else z
            y = z if y is None else y + z
        cols.append(y)
    return jnp.concatenate(cols, axis=1)


def _conv_mixer_kernel(*refs, tt, rb, n_seq, has_cache):
    if has_cache:
        (x_ref, cache_ref, w1_ref, b1_ref, wdw_ref, bdw_ref, lng_ref, lnb_ref, w2_ref, b2_ref, g_ref, b_ref,
         o_ref, nc_ref, ext_ref, act_ref) = refs
    else:
        (x_ref, w1_ref, b1_ref, wdw_ref, bdw_ref, lng_ref, lnb_ref, w2_ref, b2_ref, g_ref, b_ref,
         o_ref, nc_ref, ext_ref, act_ref) = refs
    ti = pl.program_id(1)
    d = x_ref.shape[-1]
    rows = n_seq * tt
    pad = CONV_HIST - (CONV_W - 1)

    @pl.when(ti == 0)
    def _():
        for s in range(n_seq):
            ext_ref[s, 0:CONV_HIST, :] = jnp.zeros((CONV_HIST, d), F32)
            if has_cache:
                ext_ref[s, pad:CONV_HIST, :] = cache_ref[s]

    @pl.when(ti > 0)
    def _():
        for s in range(n_seq):
            ext_ref[s, 0:CONV_HIST, :] = ext_ref[s, tt:tt + CONV_HIST, :]

    xb = x_ref[...].reshape(rows, d).astype(BF16)
    a = _dot(xb, w1_ref[:, 0:d]) + b1_ref[:, 0:d]
    gate = _dot(xb, w1_ref[:, d:]) + b1_ref[:, d:]
    u = a * jax.nn.sigmoid(gate)
    for s in range(n_seq):
        ext_ref[s, CONV_HIST:CONV_HIST + tt, :] = u[s * tt:(s + 1) * tt]

    def block(s, r0):
        y = _ln(_conv_rows(ext_ref.at[s], wdw_ref, r0, rb) + bdw_ref[...], lng_ref[...], lnb_ref[...])
        act_ref[pl.ds(s * tt + r0, rb), :] = _silu(y).astype(BF16)

    if n_seq == 1:
        def body(i, carry):
            block(0, pl.multiple_of(i * rb, rb))
            return carry
        lax.fori_loop(0, tt // rb, body, 0)
    else:
        for s in range(n_seq):
            for r0 in range(0, tt, rb):
                block(s, r0)
    mix = _dot(act_ref[...], w2_ref[...]) + b2_ref[...]
    y = _ln(ALPHA * x_ref[...].reshape(rows, d) + mix, g_ref[...], b_ref[...])
    o_ref[...] = y.reshape(n_seq, tt, d)

    @pl.when(ti == pl.num_programs(1) - 1)
    def _():
        for s in range(n_seq):
            nc_ref[s] = ext_ref[s, tt + pad:tt + CONV_HIST, :]


def _conv_mixer(x, cache, w1, b1, wdw, bdw, lng, lnb, w2, b2, g, b, bsz, t):
    m, d = x.shape
    tt = min(t, CONV_TILE)
    nt = t // tt
    rb = min(tt, CONV_ROWS)
    n_seq = 1 if nt > 1 else min(bsz, CONV_SEQS)
    assert t % tt == 0 and tt % rb == 0 and bsz % n_seq == 0
    blk = lambda bi, ti: (bi, ti, 0)
    per_seq = lambda bi, ti: (bi, 0, 0)
    vec = lambda a: a.reshape(1, -1)
    in_specs = [pl.BlockSpec((n_seq, tt, d), blk)]
    args = [x.reshape(bsz, t, d)]
    if cache is not None:
        in_specs.append(pl.BlockSpec((None, n_seq, CONV_W - 1, d), lambda bi, ti: (0, bi, 0, 0)))
        args.append(cache)
    consts = [w1, vec(b1), wdw, vec(bdw), vec(lng), vec(lnb), w2, vec(b2), vec(g), vec(b)]
    in_specs += [_resident(a.shape) for a in consts]
    y, new_cache = pl.pallas_call(
        functools.partial(_conv_mixer_kernel, tt=tt, rb=rb, n_seq=n_seq, has_cache=cache is not None),
        grid=(bsz // n_seq, nt),
        in_specs=in_specs,
        out_specs=[pl.BlockSpec((n_seq, tt, d), blk), pl.BlockSpec((n_seq, CONV_W - 1, d), per_seq)],
        out_shape=[jax.ShapeDtypeStruct((bsz, t, d), F32),
                   jax.ShapeDtypeStruct((bsz, CONV_W - 1, d), F32)],
        scratch_shapes=[pltpu.VMEM((n_seq, CONV_HIST + tt, d), F32), pltpu.VMEM((n_seq * tt, d), BF16)],
        compiler_params=_params(2),
        name="conv_mixer",
    )(*args, *consts)
    return y.reshape(m, d), new_cache


def _ffn(x, i, s, p, wb, casts=()):
    names = [f'g{i}{s}', f'u{i}{s}', f'd{i}{s}']
    g, b = p['ln_g'][i, 2 * s], p['ln_b'][i, 2 * s]
    if x.shape[0] <= FFN_SMALL_ROWS:
        if names[0] in wb:
            y, _ = _ffn_ln_small(x, *[wb[k] for k in names], (), g, b)
        else:
            y, cast = _ffn_ln_small(x, p['ffn_w_gate'], p['ffn_w_up'], p['ffn_w_down'], (i, s), g, b)
            wb.update(zip(names, cast))
        return y
    todo = [(name, src, lead) for name, src, lead in casts if name not in wb]
    y, cast = _ffn_ln(x, *[wb[k] for k in names], g, b, casts=[(src, lead) for _, src, lead in todo])
    wb.update({name: c for (name, _, _), c in zip(todo, cast)})
    return y


def _trunk(x3, hgrn_state, conv_cache, p, wb, x1=None):
    bsz, t, d = x3.shape
    ln = lambda i, s: (p['ln_g'][i, s], p['ln_b'][i, s])
    ffn = lambda x, i, s, casts: _ffn(x, i, s, p, wb, casts)

    def ffn_casts(i, s):
        return [(f'g{i}{s}', p['ffn_w_gate'], (i, s)), (f'u{i}{s}', p['ffn_w_up'], (i, s)),
                (f'd{i}{s}', p['ffn_w_down'], (i, s))]

    x = x1 if x1 is not None else ffn(
        x3.reshape(bsz * t, d), 0, 0,
        ffn_casts(0, 1) + [('w_in', p['hgrn_w_in'], (0,)), ('w_out', p['hgrn_w_out'], (0,))])
    x, h_new = _hgrn_mixer(x, wb['w_in'], p['hgrn_lb'], p['hgrn_norm_g'][0], wb['w_out'], *ln(0, 1),
                           hgrn_state, bsz, t, 0)
    x = ffn(x, 0, 1, ffn_casts(1, 0) + [('pw1', p['conv_w_pw1'], (0,)), ('pw2', p['conv_w_pw2'], (0,))])
    x = ffn(x, 1, 0, ffn_casts(1, 1))
    x, c_new = _conv_mixer(x, conv_cache, wb['pw1'], p['conv_b_pw1'][0],
                           p['conv_w_dw'][0], p['conv_b_dw'][0], p['conv_ln_g'][0], p['conv_ln_b'][0],
                           wb['pw2'], p['conv_b_pw2'][0], *ln(1, 1), bsz, t)
    x = ffn(x, 1, 1, [])
    return x.reshape(bsz, t, d), h_new[None], c_new[None]


def kernel(x_prompt, x_sample, state_hgrn, cache_conv, ffn_w_gate, ffn_w_up, ffn_w_down, ln_g, ln_b, hgrn_w_in, hgrn_lb, hgrn_norm_g, hgrn_w_out, conv_w_pw1, conv_b_pw1, conv_w_dw, conv_b_dw, conv_ln_g, conv_ln_b, conv_w_pw2, conv_b_pw2):
    assert ffn_w_gate.shape[0] == DEPTH == 2 and hgrn_w_in.shape[0] == 1 and conv_w_pw1.shape[0] == 1
    p = {'ffn_w_gate': ffn_w_gate, 'ffn_w_up': ffn_w_up, 'ffn_w_down': ffn_w_down, 'ln_g': ln_g, 'ln_b': ln_b,
         'hgrn_w_in': hgrn_w_in, 'hgrn_lb': hgrn_lb, 'hgrn_norm_g': hgrn_norm_g, 'hgrn_w_out': hgrn_w_out,
         'conv_w_pw1': conv_w_pw1, 'conv_b_pw1': conv_b_pw1, 'conv_w_dw': conv_w_dw, 'conv_b_dw': conv_b_dw,
         'conv_ln_g': conv_ln_g, 'conv_ln_b': conv_ln_b, 'conv_w_pw2': conv_w_pw2, 'conv_b_pw2': conv_b_pw2}
    wb = {}
    d = x_sample.shape[-1]
    x1_s = _ffn(x_sample.reshape(-1, d), 0, 0, p, wb)
    y_p, h_p, c_p = _trunk(x_prompt, None, None, p, wb)
    y_s, h_s, c_s = _trunk(x_sample, state_hgrn, cache_conv, p, wb, x1=x1_s)
    return (y_p, y_s, h_p, h_s, c_p, c_s)
```

```python
import functools
import math

import jax
import jax.numpy as jnp
from jax import lax
from jax.experimental import pallas as pl
from jax.experimental.pallas import tpu as pltpu

F32 = jnp.float32
BF16 = jnp.bfloat16

DEPTH = 2
N_HEADS = 8
HEAD_K = 128
HEAD_V = 128
CONV_W = 31
HGRN_BLOCK = 16
ALPHA = (2 * DEPTH) ** 0.25
LN_EPS = 1e-5
RMS_EPS = 1e-5
LOG2E = math.log2(math.e)

V7X_VMEM_LIMIT_BYTES = 56 * 1024 * 1024
LANES = 128
SUBLANES = 8
BF16_ROWS = 16
ROW_TILE = 512
CONV_TILE = 1024
CONV_HIST = 32
CONV_ROWS = 256
FFN_COLS = 512
FFN_ROW_SPLIT = 2
HGRN_GROUP = 2
FFN_SMALL_COLS = 1024
FFN_SMALL_ROWS = 256
CONV_SEQS = 16
HGRN_CHUNK = 128
HGRN_SEQS = 8
HGRN_VMEM_LIMIT_BYTES = 62 * 1024 * 1024


def _row_tile(m):
    return ROW_TILE if m % ROW_TILE == 0 else m


def _resident(shape):
    return pl.BlockSpec(shape, lambda *_: (0,) * len(shape), pipeline_mode=pl.Buffered(1))


def _params(n_axes, vmem_limit_bytes=V7X_VMEM_LIMIT_BYTES):
    return pltpu.CompilerParams(
        dimension_semantics=("arbitrary",) * n_axes,
        vmem_limit_bytes=vmem_limit_bytes)


def _ln(y, g, b):
    mu = jnp.mean(y, -1, keepdims=True)
    d = y - mu
    var = jnp.mean(d * d, -1, keepdims=True)
    return d * lax.rsqrt(var + LN_EPS) * g + b


def _silu(x):
    return x * jax.nn.sigmoid(x)


def _dot(a, b):
    return jnp.dot(a, b, preferred_element_type=F32)


def _dot_nt(a, b):
    return lax.dot_general(a, b, (((1,), (1,)), ((), ())), preferred_element_type=F32)


def _dot_tn(a, b):
    return lax.dot_general(a, b, (((0,), (0,)), ((), ())), preferred_element_type=F32)


def _cast_specs(w, lead, steps):
    rows, cols = w.shape[-2:]
    nblk = min(steps, rows // BF16_ROWS)
    rb = rows // nblk
    assert rb * nblk == rows and rb % BF16_ROWS == 0
    blk = lambda i: jnp.minimum(i, nblk - 1)
    in_spec = pl.BlockSpec((None,) * len(lead) + (rb, cols), lambda i: tuple(lead) + (blk(i), 0))
    out_spec = pl.BlockSpec((rb, cols), lambda i: (blk(i), 0))
    return in_spec, out_spec, jax.ShapeDtypeStruct((rows, cols), BF16)


def _ffn_ln_kernel(*refs, fc, n_cast):
    x_ref, wg_ref, wu_ref, wd_ref, g_ref, b_ref = refs[:6]
    cast_in = refs[6:6 + n_cast]
    o_ref = refs[6 + n_cast]
    cast_out = refs[7 + n_cast:7 + 2 * n_cast]
    h_ref = refs[7 + 2 * n_cast]
    d_ff = wg_ref.shape[1]
    half = x_ref.shape[0] // FFN_ROW_SPLIT
    for hb in range(FFN_ROW_SPLIT):
        rs = slice(hb * half, (hb + 1) * half)
        x = x_ref[rs, :]
        xb = x.astype(BF16)
        for c in range(d_ff // fc):
            sl = slice(c * fc, (c + 1) * fc)
            h_ref[rs, sl] = (_silu(_dot(xb, wg_ref[:, sl])) * _dot(xb, wu_ref[:, sl])).astype(BF16)
        y = _dot(h_ref[rs, :], wd_ref[...])
        o_ref[rs, :] = _ln(ALPHA * x + 0.5 * y, g_ref[...], b_ref[...])
    for src, dst in zip(cast_in, cast_out):
        dst[...] = src[...].astype(BF16)


def _ffn_ln(x, wg, wu, wd, g, b, casts=()):
    m, d = x.shape
    d_ff = wg.shape[1]
    tm = _row_tile(m)
    steps = m // tm
    row = lambda i: (i, 0)
    cast_specs = [_cast_specs(w, lead, steps) for w, lead in casts]
    outs = pl.pallas_call(
        functools.partial(_ffn_ln_kernel, fc=FFN_COLS, n_cast=len(casts)),
        grid=(steps,),
        in_specs=[pl.BlockSpec((tm, d), row),
                  _resident((d, d_ff)), _resident((d, d_ff)), _resident((d_ff, d)),
                  _resident((1, d)), _resident((1, d))] + [s[0] for s in cast_specs],
        out_specs=[pl.BlockSpec((tm, d), row)] + [s[1] for s in cast_specs],
        out_shape=[jax.ShapeDtypeStruct((m, d), F32)] + [s[2] for s in cast_specs],
        scratch_shapes=[pltpu.VMEM((tm, d_ff), BF16)],
        compiler_params=_params(1),
        name="ffn_ln",
    )(x, wg, wu, wd, g.reshape(1, d), b.reshape(1, d), *[w for w, _ in casts])
    return outs[0], outs[1:]


def _ffn_small_kernel(*refs, emit):
    x_ref, wg_ref, wu_ref, wd_ref, g_ref, b_ref, o_ref = refs[:7]
    acc_ref = refs[-1]
    c = pl.program_id(0)
    wg, wu, wd = wg_ref[...].astype(BF16), wu_ref[...].astype(BF16), wd_ref[...].astype(BF16)
    if emit:
        for dst, w in zip(refs[7:10], (wg, wu, wd)):
            dst[...] = w
    x = x_ref[...]
    xb = x.astype(BF16)
    part = _dot((_silu(_dot(xb, wg)) * _dot(xb, wu)).astype(BF16), wd)

    @pl.when(c == 0)
    def _():
        acc_ref[...] = part

    @pl.when(c > 0)
    def _():
        acc_ref[...] += part

    @pl.when(c == pl.num_programs(0) - 1)
    def _():
        o_ref[...] = _ln(ALPHA * x + 0.5 * acc_ref[...], g_ref[...], b_ref[...])


def _ffn_ln_small(x, wg, wu, wd, lead, g, b):
    m, d = x.shape
    d_ff = wg.shape[-1]
    fc = FFN_SMALL_COLS
    emit = wg.dtype != BF16
    sq = (None,) * len(lead)
    col = lambda c: tuple(lead) + (0, c)
    row = lambda c: tuple(lead) + (c, 0)
    out_specs = [pl.BlockSpec((m, d), lambda c: (0, 0))]
    out_shape = [jax.ShapeDtypeStruct((m, d), F32)]
    if emit:
        out_specs += [pl.BlockSpec((d, fc), lambda c: (0, c)), pl.BlockSpec((d, fc), lambda c: (0, c)),
                      pl.BlockSpec((fc, d), lambda c: (c, 0))]
        out_shape += [jax.ShapeDtypeStruct((d, d_ff), BF16), jax.ShapeDtypeStruct((d, d_ff), BF16),
                      jax.ShapeDtypeStruct((d_ff, d), BF16)]
    outs = pl.pallas_call(
        functools.partial(_ffn_small_kernel, emit=emit),
        grid=(d_ff // fc,),
        in_specs=[_resident((m, d)), pl.BlockSpec(sq + (d, fc), col), pl.BlockSpec(sq + (d, fc), col),
                  pl.BlockSpec(sq + (fc, d), row), _resident((1, d)), _resident((1, d))],
        out_specs=out_specs,
        out_shape=out_shape,
        scratch_shapes=[pltpu.VMEM((m, d), F32)],
        compiler_params=_params(1),
        name="ffn_ln_small",
    )(x, wg, wu, wd, g.reshape(1, d), b.reshape(1, d))
    return outs[0], outs[1:]


def _hgrn_operands(qs, k, b2, n):
    levels = []
    l = n // 2
    while l >= HGRN_BLOCK:
        parts = []
        for j in range(n // l):
            r = slice(j * l, (j + 1) * l)
            a = (j | 1) * l - 1
            anchor = b2[a:a + 1, :]
            parts.append(qs[r] * jnp.exp2(b2[r] - anchor) if j & 1 else k[r] * jnp.exp2(anchor - b2[r]))
        levels.append((l, jnp.concatenate(parts, axis=0).astype(BF16)))
        l //= 2
    qd, kd = [], []
    for j in range(n // HGRN_BLOCK):
        r = slice(j * HGRN_BLOCK, (j + 1) * HGRN_BLOCK)
        m = j * HGRN_BLOCK + HGRN_BLOCK // 2
        rel = b2[r] - b2[m:m + 1, :]
        qd.append(qs[r] * jnp.exp2(rel))
        kd.append(k[r] * jnp.exp2(-rel))
    cat = lambda ps: (ps[0] if len(ps) == 1 else jnp.concatenate(ps, axis=0)).astype(BF16)
    b_last = b2[n - 1:n, :]
    q_in = (qs * jnp.exp2(b2)).astype(BF16)
    k_st = (k * jnp.exp2(b_last - b2)).astype(BF16)
    return levels, cat(qd), cat(kd), q_in, k_st, jnp.exp2(b_last)


def _hgrn_mixer_kernel(*refs, chunk, n_seq, layer, has_init):
    if has_init:
        (x_ref, win_ref, lb_ref, tri_ref, ng_ref, wout_ref, g_ref, b_ref, s0_ref, o_ref, s_out_ref,
         st_ref, qs_s, lf_s, k_s, v_s, sg_s, og_s) = refs
    else:
        (x_ref, win_ref, lb_ref, tri_ref, ng_ref, wout_ref, g_ref, b_ref, o_ref, s_out_ref,
         st_ref, qs_s, lf_s, k_s, v_s, sg_s, og_s) = refs
        s0_ref = None
    c = pl.program_id(1)
    n = chunk
    rows = n_seq * n
    d = x_ref.shape[-1]
    dk = N_HEADS * HEAD_K

    @pl.when(c == 0)
    def _():
        for s in range(n_seq):
            for h in range(N_HEADS):
                st_ref[s, h] = s0_ref[s, h] if has_init else jnp.zeros((HEAD_K, HEAD_V), F32)

    xb = x_ref[...].reshape(rows, d).astype(BF16)
    lbp = lb_ref[...]
    e = jnp.exp(lbp - jnp.max(lbp, axis=0, keepdims=True))
    lb = jnp.sum(e[:layer + 1], axis=0, keepdims=True) / jnp.sum(e, axis=0, keepdims=True)
    qs_s[...] = _silu(_dot(xb, win_ref[:, 0:dk])) * (HEAD_K ** -0.5)
    forget = lb + (1.0 - lb) * jax.nn.sigmoid(_dot(xb, win_ref[:, dk:2 * dk]))
    k_s[...] = 1.0 - forget
    lf_s[...] = jnp.log(forget)
    v_s[...] = _dot(xb, win_ref[:, 2 * dk:2 * dk + d]).astype(BF16)
    sg_s[...] = _silu(_dot(xb, win_ref[:, 2 * dk + d:])).astype(BF16)

    t_idx = lax.broadcasted_iota(jnp.int32, (n, n), 0)
    s_idx = lax.broadcasted_iota(jnp.int32, (n, n), 1)
    m_d = ((t_idx // HGRN_BLOCK) == (s_idx // HGRN_BLOCK)) & (s_idx <= t_idx)
    masks = {}
    l = n // 2
    while l >= HGRN_BLOCK:
        masks[l] = (((t_idx // l) & 1) == 1) & ((s_idx // l) == (t_idx // l) - 1)
        l //= 2
    ng = ng_ref[...]
    tri = tri_ref[...]
    heads = [(slice(h * HEAD_K, (h + 1) * HEAD_K), slice(h * HEAD_V, (h + 1) * HEAD_V)) for h in range(N_HEADS)]

    def recur(seqs):
        ops = {}
        for s in seqs:
            r = slice(s * n, (s + 1) * n)
            lf = lf_s[r, :]
            lf_hi = lf.astype(BF16)
            r1 = lf - lf_hi.astype(F32)
            lf_mid = r1.astype(BF16)
            lf_lo = (r1 - lf_mid.astype(F32)).astype(BF16)
            b2 = (_dot(tri, lf_hi) + _dot(tri, lf_mid) + _dot(tri, lf_lo)) * LOG2E
            ops[s] = _hgrn_operands(qs_s[r, :], k_s[r, :], b2, n) + (v_s[r, :],)
        units = [(s, h, sl, sv) for s in seqs for h, (sl, sv) in enumerate(heads)]
        scores = []
        for s, h, sl, sv in units:
            levels, q_d, k_d = ops[s][:3]
            sc = jnp.where(m_d, _dot_nt(q_d[:, sl], k_d[:, sl]), 0.0)
            for l, x_l in levels:
                sc = jnp.where(masks[l], _dot_nt(x_l[:, sl], x_l[:, sl]), sc)
            scores.append(sc.astype(BF16))
        outs = [_dot(jnp.concatenate([sc, ops[s][3][:, sl]], axis=1),
                     jnp.concatenate([ops[s][6][:, sv], st_ref[s, h].astype(BF16)], axis=0))
                for (s, h, sl, sv), sc in zip(units, scores)]
        for s, h, sl, sv in units:
            decay_col = jnp.broadcast_to(ops[s][5][:, sl], (HEAD_V, HEAD_K)).T
            st_ref[s, h] = st_ref[s, h] * decay_col + _dot_tn(ops[s][4][:, sl], ops[s][6][:, sv])
        for (s, h, sl, sv), o in zip(units, outs):
            r = slice(s * n, (s + 1) * n)
            o = o * lax.rsqrt(jnp.mean(o * o, -1, keepdims=True) + RMS_EPS) * ng
            og_s[r, sv] = (o * sg_s[r, sv]).astype(BF16)

    group, rec = (min(n_seq, HGRN_GROUP),) * 2 if n == HGRN_CHUNK else (n_seq, 1)
    for s0 in range(0, n_seq, group):
        for s in range(s0, s0 + group, rec):
            recur(range(s, s + rec))
        rg = slice(s0 * n, (s0 + group) * n)
        x = x_ref[s0:s0 + group].reshape(group * n, d)
        y = _ln(ALPHA * x + _dot(og_s[rg, :], wout_ref[...]), g_ref[...], b_ref[...])
        o_ref[s0:s0 + group] = y.reshape(group, n, d)

    @pl.when(c == pl.num_programs(1) - 1)
    def _():
        for s in range(n_seq):
            for h in range(N_HEADS):
                s_out_ref[s, h] = st_ref[s, h]


def _hgrn_mixer(x, w_in, lb_param, norm_g, w_out, g, b, s0, bsz, t, layer):
    m, d = x.shape
    dk = N_HEADS * HEAD_K
    chunk = min(t, HGRN_CHUNK)
    nc = t // chunk
    n_seq = min(bsz, HGRN_SEQS)
    assert bsz % n_seq == 0 and t % chunk == 0
    rows = n_seq * chunk
    blk = lambda bi, ci: (bi, ci, 0)
    st = lambda bi, ci: (bi, 0, 0, 0)
    tri = jnp.tril(jnp.ones((chunk, chunk), BF16))
    in_specs = [pl.BlockSpec((n_seq, chunk, d), blk), _resident(w_in.shape), _resident(lb_param.shape),
                _resident(tri.shape), _resident((1, HEAD_V)), _resident(w_out.shape),
                _resident((1, d)), _resident((1, d))]
    args = [x.reshape(bsz, t, d), w_in, lb_param, tri, norm_g.reshape(1, HEAD_V), w_out,
            g.reshape(1, d), b.reshape(1, d)]
    if s0 is not None:
        in_specs.append(pl.BlockSpec((None, n_seq, N_HEADS, HEAD_K, HEAD_V), lambda bi, ci: (0, bi, 0, 0, 0)))
        args.append(s0)
    y, s_fin = pl.pallas_call(
        functools.partial(_hgrn_mixer_kernel, chunk=chunk, n_seq=n_seq, layer=layer, has_init=s0 is not None),
        grid=(bsz // n_seq, nc),
        in_specs=in_specs,
        out_specs=[pl.BlockSpec((n_seq, chunk, d), blk),
                   pl.BlockSpec((n_seq, N_HEADS, HEAD_K, HEAD_V), st)],
        out_shape=[jax.ShapeDtypeStruct((bsz, t, d), F32),
                   jax.ShapeDtypeStruct((bsz, N_HEADS, HEAD_K, HEAD_V), F32)],
        scratch_shapes=[pltpu.VMEM((n_seq, N_HEADS, HEAD_K, HEAD_V), F32),
                        pltpu.VMEM((rows, dk), F32), pltpu.VMEM((rows, dk), F32), pltpu.VMEM((rows, dk), F32),
                        pltpu.VMEM((rows, d), BF16), pltpu.VMEM((rows, d), BF16), pltpu.VMEM((rows, d), BF16)],
        compiler_params=_params(2, HGRN_VMEM_LIMIT_BYTES),
        name="hgrn_mixer",
    )(*args)
    return y.reshape(m, d), s_fin


def _conv_rows(ext_ref, wdw_ref, r0, rb):
    pad = CONV_HIST - (CONV_W - 1)
    cols = []
    for lb in range(ext_ref.shape[1] // LANES):
        ls = slice(lb * LANES, (lb + 1) * LANES)
        win = ext_ref[pl.ds(r0, rb + CONV_HIST), ls]
        y = None
        for r in range(SUBLANES):
            rows = rb + (SUBLANES if r else 0)
            z = None
            for a in range(CONV_HIST // SUBLANES + 1):
                j = SUBLANES * a + r - pad
                if 0 <= j < CONV_W:
                    term = wdw_ref[j:j + 1, ls] * win[SUBLANES * a:SUBLANES * a + rows, :]
                    z = term if z is None else z + term
            z = z[r:r + rb, :] if r else z
            y = z if y is None else y + z
        cols.append(y)
    return jnp.concatenate(cols, axis=1)


def _conv_mixer_kernel(*refs, tt, rb, n_seq, has_cache):
    if has_cache:
        (x_ref, cache_ref, w1_ref, b1_ref, wdw_ref, bdw_ref, lng_ref, lnb_ref, w2_ref, b2_ref, g_ref, b_ref,
         o_ref, nc_ref, ext_ref, act_ref) = refs
    else:
        (x_ref, w1_ref, b1_ref, wdw_ref, bdw_ref, lng_ref, lnb_ref, w2_ref, b2_ref, g_ref, b_ref,
         o_ref, nc_ref, ext_ref, act_ref) = refs
    ti = pl.program_id(1)
    d = x_ref.shape[-1]
    rows = n_seq * tt
    pad = CONV_HIST - (CONV_W - 1)

    @pl.when(ti == 0)
    def _():
        for s in range(n_seq):
            ext_ref[s, 0:CONV_HIST, :] = jnp.zeros((CONV_HIST, d), F32)
            if has_cache:
                ext_ref[s, pad:CONV_HIST, :] = cache_ref[s]

    @pl.when(ti > 0)
    def _():
        for s in range(n_seq):
            ext_ref[s, 0:CONV_HIST, :] = ext_ref[s, tt:tt + CONV_HIST, :]

    xb = x_ref[...].reshape(rows, d).astype(BF16)
    a = _dot(xb, w1_ref[:, 0:d]) + b1_ref[:, 0:d]
    gate = _dot(xb, w1_ref[:, d:]) + b1_ref[:, d:]
    u = a * jax.nn.sigmoid(gate)
    for s in range(n_seq):
        ext_ref[s, CONV_HIST:CONV_HIST + tt, :] = u[s * tt:(s + 1) * tt]

    def block(s, r0):
        y = _ln(_conv_rows(ext_ref.at[s], wdw_ref, r0, rb) + bdw_ref[...], lng_ref[...], lnb_ref[...])
        act_ref[pl.ds(s * tt + r0, rb), :] = _silu(y).astype(BF16)

    if n_seq == 1:
        def body(i, carry):
            block(0, pl.multiple_of(i * rb, rb))
            return carry
        lax.fori_loop(0, tt // rb, body, 0)
    else:
        for s in range(n_seq):
            for r0 in range(0, tt, rb):
                block(s, r0)
    mix = _dot(act_ref[...], w2_ref[...]) + b2_ref[...]
    y = _ln(ALPHA * x_ref[...].reshape(rows, d) + mix, g_ref[...], b_ref[...])
    o_ref[...] = y.reshape(n_seq, tt, d)

    @pl.when(ti == pl.num_programs(1) - 1)
    def _():
        for s in range(n_seq):
            nc_ref[s] = ext_ref[s, tt + pad:tt + CONV_HIST, :]


def _conv_mixer(x, cache, w1, b1, wdw, bdw, lng, lnb, w2, b2, g, b, bsz, t):
    m, d = x.shape
    tt = min(t, CONV_TILE)
    nt = t // tt
    rb = min(tt, CONV_ROWS)
    n_seq = 1 if nt > 1 else min(bsz, CONV_SEQS)
    assert t % tt == 0 and tt % rb == 0 and bsz % n_seq == 0
    blk = lambda bi, ti: (bi, ti, 0)
    per_seq = lambda bi, ti: (bi, 0, 0)
    vec = lambda a: a.reshape(1, -1)
    in_specs = [pl.BlockSpec((n_seq, tt, d), blk)]
    args = [x.reshape(bsz, t, d)]
    if cache is not None:
        in_specs.append(pl.BlockSpec((None, n_seq, CONV_W - 1, d), lambda bi, ti: (0, bi, 0, 0)))
        args.append(cache)
    consts = [w1, vec(b1), wdw, vec(bdw), vec(lng), vec(lnb), w2, vec(b2), vec(g), vec(b)]
    in_specs += [_resident(a.shape) for a in consts]
    y, new_cache = pl.pallas_call(
        functools.partial(_conv_mixer_kernel, tt=tt, rb=rb, n_seq=n_seq, has_cache=cache is not None),
        grid=(bsz // n_seq, nt),
        in_specs=in_specs,
        out_specs=[pl.BlockSpec((n_seq, tt, d), blk), pl.BlockSpec((n_seq, CONV_W - 1, d), per_seq)],
        out_shape=[jax.ShapeDtypeStruct((bsz, t, d), F32),
                   jax.ShapeDtypeStruct((bsz, CONV_W - 1, d), F32)],
        scratch_shapes=[pltpu.VMEM((n_seq, CONV_HIST + tt, d), F32), pltpu.VMEM((n_seq * tt, d), BF16)],
        compiler_params=_params(2),
        name="conv_mixer",
    )(*args, *consts)
    return y.reshape(m, d), new_cache


def _ffn(x, i, s, p, wb, casts=()):
    names = [f'g{i}{s}', f'u{i}{s}', f'd{i}{s}']
    g, b = p['ln_g'][i, 2 * s], p['ln_b'][i, 2 * s]
    if x.shape[0] <= FFN_SMALL_ROWS:
        if names[0] in wb:
            y, _ = _ffn_ln_small(x, *[wb[k] for k in names], (), g, b)
        else:
            y, cast = _ffn_ln_small(x, p['ffn_w_gate'], p['ffn_w_up'], p['ffn_w_down'], (i, s), g, b)
            wb.update(zip(names, cast))
        return y
    todo = [(name, src, lead) for name, src, lead in casts if name not in wb]
    y, cast = _ffn_ln(x, *[wb[k] for k in names], g, b, casts=[(src, lead) for _, src, lead in todo])
    wb.update({name: c for (name, _, _), c in zip(todo, cast)})
    return y


def _trunk(x3, hgrn_state, conv_cache, p, wb, x1=None):
    bsz, t, d = x3.shape
    ln = lambda i, s: (p['ln_g'][i, s], p['ln_b'][i, s])
    ffn = lambda x, i, s, casts: _ffn(x, i, s, p, wb, casts)

    def ffn_casts(i, s):
        return [(f'g{i}{s}', p['ffn_w_gate'], (i, s)), (f'u{i}{s}', p['ffn_w_up'], (i, s)),
                (f'd{i}{s}', p['ffn_w_down'], (i, s))]

    x = x1 if x1 is not None else ffn(
        x3.reshape(bsz * t, d), 0, 0,
        ffn_casts(0, 1) + [('w_in', p['hgrn_w_in'], (0,)), ('w_out', p['hgrn_w_out'], (0,))])
    x, h_new = _hgrn_mixer(x, wb['w_in'], p['hgrn_lb'], p['hgrn_norm_g'][0], wb['w_out'], *ln(0, 1),
                           hgrn_state, bsz, t, 0)
    x = ffn(x, 0, 1, ffn_casts(1, 0) + [('pw1', p['conv_w_pw1'], (0,)), ('pw2', p['conv_w_pw2'], (0,))])
    x = ffn(x, 1, 0, ffn_casts(1, 1))
    x, c_new = _conv_mixer(x, conv_cache, wb['pw1'], p['conv_b_pw1'][0],
                           p['conv_w_dw'][0], p['conv_b_dw'][0], p['conv_ln_g'][0], p['conv_ln_b'][0],
                           wb['pw2'], p['conv_b_pw2'][0], *ln(1, 1), bsz, t)
    x = ffn(x, 1, 1, [])
    return x.reshape(bsz, t, d), h_new[None], c_new[None]


def kernel(x_prompt, x_sample, state_hgrn, cache_conv, ffn_w_gate, ffn_w_up, ffn_w_down, ln_g, ln_b, hgrn_w_in, hgrn_lb, hgrn_norm_g, hgrn_w_out, conv_w_pw1, conv_b_pw1, conv_w_dw, conv_b_dw, conv_ln_g, conv_ln_b, conv_w_pw2, conv_b_pw2):
    assert ffn_w_gate.shape[0] == DEPTH == 2 and hgrn_w_in.shape[0] == 1 and conv_w_pw1.shape[0] == 1
    p = {'ffn_w_gate': ffn_w_gate, 'ffn_w_up': ffn_w_up, 'ffn_w_down': ffn_w_down, 'ln_g': ln_g, 'ln_b': ln_b,
         'hgrn_w_in': hgrn_w_in, 'hgrn_lb': hgrn_lb, 'hgrn_norm_g': hgrn_norm_g, 'hgrn_w_out': hgrn_w_out,
         'conv_w_pw1': conv_w_pw1, 'conv_b_pw1': conv_b_pw1, 'conv_w_dw': conv_w_dw, 'conv_b_dw': conv_b_dw,
         'conv_ln_g': conv_ln_g, 'conv_ln_b': conv_ln_b, 'conv_w_pw2': conv_w_pw2, 'conv_b_pw2': conv_b_pw2}
    wb = {}
    d = x_sample.shape[-1]
    x1_s = _ffn(x_sample.reshape(-1, d), 0, 0, p, wb)
    y_p, h_p, c_p = _trunk(x_prompt, None, None, p, wb)
    y_s, h_s, c_s = _trunk(x_sample, state_hgrn, cache_conv, p, wb, x1=x1_s)
    return (y_p, y_s, h_p, h_s, c_p, c_s)
```
